```python
import jax, jax.numpy as jnp
from jax import lax
import numpy as np

D_MODEL = 4096
BATCH = 2
SEQ = 8192
DEPTH = 2

GRID_W = 64
CTX_LEN = 256
HEAD_DIM = 128
N_MIX_HEADS = D_MODEL // HEAD_DIM

GM_GROUPS = N_MIX_HEADS // 2
GM_GROUP_DIM = HEAD_DIM
GM_WIDTH = GM_GROUPS * GM_GROUP_DIM
CHUNK = 128

SWA_HEADS = N_MIX_HEADS // 2
SWA_KV_HEADS = 4
SWA_WIDTH = SWA_HEADS * HEAD_DIM
SWA_KV_WIDTH = SWA_KV_HEADS * HEAD_DIM
WINDOW = 128
BLOCK = 128

AB_SPLITS = (2 * GM_WIDTH, 2 * GM_WIDTH + SWA_WIDTH, 2 * GM_WIDTH + SWA_WIDTH + SWA_KV_WIDTH)
AB_IN = 2 * GM_WIDTH + SWA_WIDTH + 2 * SWA_KV_WIDTH
AB_OUT = GM_WIDTH + SWA_WIDTH

NAT_HEADS = N_MIX_HEADS
NAT_WIDTH = NAT_HEADS * HEAD_DIM
NA_KH = 8
NA_KW = 16

FFN_HIDDEN = 2 * D_MODEL
HALF_STEP = 0.5
N_MOD = 9
ADA_STD = 0.5
ROPE_THETA = 10000.0
NORM_EPS = 1e-6
NEG_INF = -1e30
N_EVEN = (DEPTH + 1) // 2
N_ODD = DEPTH // 2

kernel_name = "hybrid_gmlp_swa_natten_macaron_dit"


def rms_norm(x, g):
    xf = x.astype(jnp.float32)
    y = xf * lax.rsqrt(jnp.mean(xf * xf, -1, keepdims=True) + NORM_EPS)
    return (y * g.astype(jnp.float32)).astype(x.dtype)


def modulate(x, g, mm):
    return rms_norm(x, g) * (1 + mm[:, 1]) + mm[:, 0]


def swiglu(h, wg, wu, wd):
    return (jax.nn.silu(h @ wg) * (h @ wu)) @ wd


def ffn_half_step(x, mm, g, wg, wu, wd):
    return x + HALF_STEP * mm[:, 2] * swiglu(modulate(x, g, mm), wg, wu, wd)


def _heads(t, n):
    return t.reshape(t.shape[0], t.shape[1], n, HEAD_DIM)


def _rope_1d(x, ang):
    n = x.shape[-1] // 2
    cos, sin = jnp.cos(ang)[:, None, :], jnp.sin(ang)[:, None, :]
    xa, xb = x[..., :n], x[..., n:]
    return jnp.concatenate([xa * cos - xb * sin, xb * cos + xa * sin], -1)


def axial_rope(x):
    L, d = x.shape[1], x.shape[-1]
    half = d // 2
    t = jnp.arange(L)
    rows = (t // GRID_W).astype(jnp.float32)
    cols = (t % GRID_W).astype(jnp.float32)
    inv = 1.0 / (ROPE_THETA ** (jnp.arange(0, half, 2, dtype=jnp.float32) / half))
    xf = x.astype(jnp.float32)
    y = jnp.concatenate([_rope_1d(xf[..., :half], rows[:, None] * inv),
                         _rope_1d(xf[..., half:], cols[:, None] * inv)], -1)
    return y.astype(x.dtype)


def chunk_spatial_gating(uv, vnorm_g, w_s, b_s):
    Bn, L, _ = uv.shape
    u, v = jnp.split(uv, 2, -1)
    v = v.reshape(Bn, L // CHUNK, CHUNK, GM_GROUPS, GM_GROUP_DIM)
    vf = v.astype(jnp.float32)
    vf = vf - jnp.mean(vf, -1, keepdims=True)
    vn = (vf * lax.rsqrt(jnp.mean(vf * vf, -1, keepdims=True) + NORM_EPS)).astype(uv.dtype)
    vn = vn * vnorm_g.reshape(GM_GROUPS, GM_GROUP_DIM)
    s = jnp.einsum('gpq,bnqgc->bnpgc', w_s, vn) + b_s.T[None, None, :, :, None]
    return u * s.reshape(Bn, L, GM_WIDTH)


def window_attention(q, k, v, kc, vc, sink):
    Bn, L, H, d = q.shape
    hkv = k.shape[2]
    G = H // hkv
    nb = L // BLOCK
    scale = d ** -0.5
    qb = q.reshape(Bn, nb, BLOCK, hkv, G, d)

    def bands(t):
        tb = jnp.pad(t, ((0, 0), (BLOCK, BLOCK), (0, 0), (0, 0))).reshape(Bn, nb + 2, BLOCK, hkv, d)
        return jnp.concatenate([tb[:, :-2], tb[:, 1:-1], tb[:, 2:]], axis=2)

    kb, vb = bands(k), bands(v)
    s_win = jnp.einsum('bnqhgd,bnkhd->bhgnqk', qb, kb, preferred_element_type=jnp.float32) * scale
    qi = jnp.arange(nb)[:, None, None] * BLOCK + jnp.arange(BLOCK)[None, :, None]
    kj = jnp.arange(nb)[:, None, None] * BLOCK - BLOCK + jnp.arange(3 * BLOCK)[None, None, :]
    valid = (jnp.abs(kj - qi) <= WINDOW) & (kj >= 0) & (kj < L)
    s_win = jnp.where(valid, s_win, NEG_INF)
    s_ctx = jnp.einsum('bnqhgd,bkhd->bhgnqk', qb, kc, preferred_element_type=jnp.float32) * scale
    s_sink = jnp.broadcast_to(sink.astype(jnp.float32).reshape(1, hkv, G, 1, 1, 1), s_win.shape[:-1] + (1,))
    p = jax.nn.softmax(jnp.concatenate([s_win, s_ctx, s_sink], -1), -1)
    nw, lc = 3 * BLOCK, kc.shape[1]
    o = (jnp.einsum('bhgnqk,bnkhd->bnqhgd', p[..., :nw].astype(v.dtype), vb)
         + jnp.einsum('bhgnqk,bkhd->bnqhgd', p[..., nw:nw + lc].astype(v.dtype), vc))
    return o.reshape(Bn, L, H * d)


def context_attention(qc, kc, vc, sink):
    Bn, Lc, H, d = qc.shape
    hkv = kc.shape[2]
    G = H // hkv
    q = qc.reshape(Bn, Lc, hkv, G, d)
    s = jnp.einsum('bqhgd,bkhd->bhgqk', q, kc, preferred_element_type=jnp.float32) * d ** -0.5
    if sink is not None:
        s = jnp.concatenate([s, jnp.broadcast_to(sink.astype(jnp.float32).reshape(1, hkv, G, 1, 1),
                                                 s.shape[:-1] + (1,))], -1)
    p = jax.nn.softmax(s, -1)[..., :Lc].astype(vc.dtype)
    return jnp.einsum('bhgqk,bkhd->bqhgd', p, vc).reshape(Bn, Lc, H * d)


def neighbourhood_attention(q, k, v, kc, vc, rpb):
    Bn, L, H, d = q.shape
    rows = L // GRID_W
    kh = min(NA_KH, rows)
    scale = d ** -0.5
    qg = q.reshape(Bn, rows, GRID_W, H, d)
    kg = k.reshape(Bn, rows, GRID_W, H, d)
    vg = v.reshape(Bn, rows, GRID_W, H, d)
    col = jnp.arange(GRID_W)
    c0 = jnp.clip(col - NA_KW // 2, 0, GRID_W - NA_KW)
    key_cols = c0[:, None] + jnp.arange(NA_KW)[None, :]
    dc = key_cols - col[:, None] + (NA_KW - 1)
    n_nb = kh * NA_KW

    def row_block(r):
        r0 = jnp.clip(r - kh // 2, 0, rows - kh)
        kn = lax.dynamic_slice_in_dim(kg, r0, kh, axis=1)[:, :, key_cols]
        vn = lax.dynamic_slice_in_dim(vg, r0, kh, axis=1)[:, :, key_cols]
        qr = lax.dynamic_index_in_dim(qg, r, axis=1, keepdims=False)
        dr = r0 + jnp.arange(kh) - r + (NA_KH - 1)
        bias = rpb[:, dr[:, None, None], dc[None, :, :]].transpose(0, 2, 1, 3)
        s_nb = jnp.einsum('bwhd,bawkhd->bhwak', qr, kn, preferred_element_type=jnp.float32) * scale
        s_nb = s_nb + bias.astype(jnp.float32)[None]
        s_ctx = jnp.einsum('bwhd,bchd->bhwc', qr, kc, preferred_element_type=jnp.float32) * scale
        p = jax.nn.softmax(jnp.concatenate([s_nb.reshape(Bn, H, GRID_W, n_nb), s_ctx], -1), -1)
        p_nb = p[..., :n_nb].reshape(Bn, H, GRID_W, kh, NA_KW).astype(v.dtype)
        return (jnp.einsum('bhwak,bawkhd->bwhd', p_nb, vn)
                + jnp.einsum('bhwc,bchd->bwhd', p[..., n_nb:].astype(v.dtype), vc))

    out = lax.map(row_block, jnp.arange(rows))
    return out.transpose(1, 0, 2, 3, 4).reshape(Bn, L, H * d)


def ab_mixer(h, hc, w_in, w_out, vnorm_g, w_s, b_s, qn_g, kn_g, sink, ctx_out):
    uv, q, k, v = jnp.split(h @ w_in, AB_SPLITS, -1)
    uvc, qc, kc, vc = jnp.split(hc @ w_in, AB_SPLITS, -1)
    q = axial_rope(rms_norm(_heads(q, SWA_HEADS), qn_g))
    k = axial_rope(rms_norm(_heads(k, SWA_KV_HEADS), kn_g))
    kc = rms_norm(_heads(kc, SWA_KV_HEADS), kn_g)
    vc = _heads(vc, SWA_KV_HEADS)
    y = jnp.concatenate([chunk_spatial_gating(jax.nn.gelu(uv), vnorm_g, w_s, b_s),
                         window_attention(q, k, _heads(v, SWA_KV_HEADS), kc, vc, sink)], -1) @ w_out
    if not ctx_out:
        return y, None
    qc = rms_norm(_heads(qc, SWA_HEADS), qn_g)
    yc = jnp.concatenate([chunk_spatial_gating(jax.nn.gelu(uvc), vnorm_g, w_s, b_s),
                          context_attention(qc, kc, vc, sink)], -1) @ w_out
    return y, yc


def nat_mixer(h, hc, w_qkv, w_out, qn_g, kn_g, rpb, ctx_out):
    q, k, v = jnp.split(h @ w_qkv, 3, -1)
    qc, kc, vc = jnp.split(hc @ w_qkv, 3, -1)
    kc = rms_norm(_heads(kc, NAT_HEADS), kn_g)
    vc = _heads(vc, NAT_HEADS)
    y = neighbourhood_attention(rms_norm(_heads(q, NAT_HEADS), qn_g), rms_norm(_heads(k, NAT_HEADS), kn_g),
                                _heads(v, NAT_HEADS), kc, vc, rpb) @ w_out
    if not ctx_out:
        return y, None
    yc = context_attention(rms_norm(_heads(qc, NAT_HEADS), qn_g), kc, vc, None) @ w_out
    return y, yc


def setup_inputs(seed: int = 0) -> dict:
    key = jax.random.key(seed)
    ks = jax.random.split(key, 24)
    D = D_MODEL

    def nrm(k, shape, s):
        return jax.random.normal(k, shape, jnp.float32) * s

    return {
        "x": nrm(ks[0], (BATCH, SEQ, D), 1.0),
        "c": nrm(ks[1], (BATCH, D), 1.0),
        "ctx": nrm(ks[2], (BATCH, CTX_LEN, D), 1.0),
        "c_ctx": nrm(ks[3], (D,), 1.0),
        "w_ada": nrm(ks[4], (DEPTH, D, N_MOD * D), ADA_STD * D ** -0.5),
        "b_ada": nrm(ks[5], (DEPTH, N_MOD * D), 0.02),
        "norm_g": 1.0 + nrm(ks[6], (DEPTH, 3, D), 0.02),
        "ffn_w_gate": nrm(ks[7], (DEPTH, 2, D, FFN_HIDDEN), D ** -0.5),
        "ffn_w_up": nrm(ks[8], (DEPTH, 2, D, FFN_HIDDEN), D ** -0.5),
        "ffn_w_down": nrm(ks[9], (DEPTH, 2, FFN_HIDDEN, D), FFN_HIDDEN ** -0.5),
        "mix_ab_w_in": nrm(ks[10], (N_EVEN, D, AB_IN), D ** -0.5),
        "mix_ab_w_out": nrm(ks[11], (N_EVEN, AB_OUT, D), AB_OUT ** -0.5),
        "gm_vnorm_g": 1.0 + nrm(ks[12], (N_EVEN, GM_WIDTH), 0.02),
        "gm_w_s": nrm(ks[13], (N_EVEN, GM_GROUPS, CHUNK, CHUNK), CHUNK ** -0.5),
        "gm_b_s": nrm(ks[14], (N_EVEN, GM_GROUPS, CHUNK), 0.02),
        "swa_qnorm_g": 1.0 + nrm(ks[15], (N_EVEN, HEAD_DIM), 0.02),
        "swa_knorm_g": 1.0 + nrm(ks[16], (N_EVEN, HEAD_DIM), 0.02),
        "swa_sink": nrm(ks[17], (N_EVEN, SWA_HEADS), 0.5),
        "nat_w_qkv": nrm(ks[18], (N_ODD, D, 3 * NAT_WIDTH), D ** -0.5),
        "nat_w_out": nrm(ks[19], (N_ODD, NAT_WIDTH, D), NAT_WIDTH ** -0.5),
        "nat_qnorm_g": 1.0 + nrm(ks[20], (N_ODD, HEAD_DIM), 0.02),
        "nat_knorm_g": 1.0 + nrm(ks[21], (N_ODD, HEAD_DIM), 0.02),
        "nat_rpb": nrm(ks[22], (N_ODD, NAT_HEADS, 2 * NA_KH - 1, 2 * NA_KW - 1), 0.1),
    }


def reference(x, c, ctx, c_ctx, w_ada, b_ada, norm_g, ffn_w_gate, ffn_w_up, ffn_w_down,
              mix_ab_w_in, mix_ab_w_out, gm_vnorm_g, gm_w_s, gm_b_s, swa_qnorm_g, swa_knorm_g, swa_sink,
              nat_w_qkv, nat_w_out, nat_qnorm_g, nat_knorm_g, nat_rpb):
    for layer in range(DEPTH):
        last = layer == DEPTH - 1
        j = layer // 2
        m = (jax.nn.silu(c) @ w_ada[layer] + b_ada[layer]).reshape(-1, 3, 3, 1, D_MODEL)
        mc = (jax.nn.silu(c_ctx)[None] @ w_ada[layer] + b_ada[layer]).reshape(1, 3, 3, 1, D_MODEL)
        x = ffn_half_step(x, m[:, 0], norm_g[layer, 0], ffn_w_gate[layer, 0], ffn_w_up[layer, 0], ffn_w_down[layer, 0])
        ctx = ffn_half_step(ctx, mc[:, 0], norm_g[layer, 0], ffn_w_gate[layer, 0], ffn_w_up[layer, 0], ffn_w_down[layer, 0])
        h = modulate(x, norm_g[layer, 1], m[:, 1])
        hc = modulate(ctx, norm_g[layer, 1], mc[:, 1])
        if layer % 2 == 0:
            y, yc = ab_mixer(h, hc, mix_ab_w_in[j], mix_ab_w_out[j], gm_vnorm_g[j], gm_w_s[j], gm_b_s[j],
                             swa_qnorm_g[j], swa_knorm_g[j], swa_sink[j], not last)
        else:
            y, yc = nat_mixer(h, hc, nat_w_qkv[j], nat_w_out[j], nat_qnorm_g[j], nat_knorm_g[j],
                              nat_rpb[j], not last)
        x = x + m[:, 1, 2] * y
        x = ffn_half_step(x, m[:, 2], norm_g[layer, 2], ffn_w_gate[layer, 1], ffn_w_up[layer, 1], ffn_w_down[layer, 1])
        if not last:
            ctx = ctx + mc[:, 1, 2] * yc
            ctx = ffn_half_step(ctx, mc[:, 2], norm_g[layer, 2], ffn_w_gate[layer, 1], ffn_w_up[layer, 1], ffn_w_down[layer, 1])
    return x
```

```python
import functools
import math

import jax
import jax.numpy as jnp
from jax import lax
from jax.experimental import pallas as pl
from jax.experimental.pallas import tpu as pltpu

F32 = jnp.float32
BF16 = jnp.bfloat16

HEAD_DIM = 128
GRID_W = 64
CHUNK = 128
WINDOW = 128
BLOCK = 128
SWA_KV_HEADS = 4
NA_KH = 8
NA_KW = 16
N_MOD = 9
HALF_STEP = 0.5
ROPE_THETA = 10000.0
NORM_EPS = 1e-6
NEG_INF = -1e30
ATTN_SCALE = HEAD_DIM ** -0.5

V7X_VMEM_BYTES = 64 * 1024 * 1024
VMEM_CEILING = V7X_VMEM_BYTES - 8 * 1024 * 1024
MOD_ROWS = 8


def _vmem_limit(pipelined_bytes, temp_bytes=0):
    need = int((2 * pipelined_bytes + temp_bytes) * 1.25) + (2 << 20)
    return min(max(need, 16 << 20), VMEM_CEILING)


def _params(semantics, pipelined_bytes, temp_bytes=0):
    return pltpu.CompilerParams(dimension_semantics=semantics,
                                vmem_limit_bytes=_vmem_limit(pipelined_bytes, temp_bytes))


def _tile(n, pref, quantum):
    if n <= pref:
        return n
    t = (pref // quantum) * quantum
    while t > quantum and n % t:
        t -= quantum
    assert n % t == 0, (n, pref, quantum)
    return t


def _nbytes(shape, dtype):
    return math.prod(shape) * jnp.dtype(dtype).itemsize


def _ada_kernel(c_ref, w_ref, b_ref, o_ref):
    c = c_ref[...]
    s = (c * jax.nn.sigmoid(c)).astype(BF16)
    o_ref[0] = jnp.dot(s, w_ref[0].astype(BF16), preferred_element_type=F32) + b_ref[0]


def _ada(cond, w_ada, b_ada):
    depth, d, n = w_ada.shape
    tn = _tile(n, 512, 128)
    blocks = _nbytes((MOD_ROWS, d), F32) + _nbytes((d, tn), F32) + 2 * _nbytes((MOD_ROWS, tn), F32)
    return pl.pallas_call(
        _ada_kernel,
        grid=(depth, n // tn),
        in_specs=[pl.BlockSpec((MOD_ROWS, d), lambda l, j: (0, 0)),
                  pl.BlockSpec((1, d, tn), lambda l, j: (l, 0, j)),
                  pl.BlockSpec((1, 1, tn), lambda l, j: (l, 0, j))],
        out_specs=pl.BlockSpec((1, MOD_ROWS, tn), lambda l, j: (l, 0, j)),
        out_shape=jax.ShapeDtypeStruct((depth, MOD_ROWS, n), F32),
        compiler_params=_params(("parallel", "parallel"), blocks, _nbytes((d, tn), BF16)),
        name="ada_modulation",
    )(cond, w_ada, b_ada.reshape(depth, 1, n))


def _norm_mod_kernel(x_ref, g_ref, shift_ref, scale_ref, o_ref):
    x = x_ref[...]
    y = x * lax.rsqrt(jnp.mean(x * x, axis=-1, keepdims=True) + NORM_EPS)
    o_ref[...] = ((y * g_ref[...]) * (1.0 + scale_ref[0]) + shift_ref[0]).astype(o_ref.dtype)


class _ModRows:
    def __init__(self, first, span):
        self.first, self.span = first, span

    def of_tile(self, i, tm):
        assert self.span % tm == 0
        return self.first + (i * tm) // self.span


def _norm_mod(x, g, shift, scale, mod_rows):
    t, d = x.shape
    tm = _tile(mod_rows.span, 256, 8)
    mod_spec = pl.BlockSpec((1, 1, d), lambda i: (mod_rows.of_tile(i, tm), 0, 0))
    blocks = _nbytes((tm, d), F32) + _nbytes((tm, d), BF16) + 3 * _nbytes((1, d), F32)
    return pl.pallas_call(
        _norm_mod_kernel,
        grid=(t // tm,),
        in_specs=[pl.BlockSpec((tm, d), lambda i: (i, 0)),
                  pl.BlockSpec((1, d), lambda i: (0, 0)),
                  mod_spec, mod_spec],
        out_specs=pl.BlockSpec((tm, d), lambda i: (i, 0)),
        out_shape=jax.ShapeDtypeStruct((t, d), BF16),
        compiler_params=_params(("parallel",), blocks, 2 * _nbytes((tm, d), F32)),
        name="norm_modulate",
    )(x, g.reshape(1, d), shift, scale)


def _matmul_kernel(a_ref, w_ref, o_ref):
    o_ref[...] = jnp.dot(a_ref[...], w_ref[...], preferred_element_type=F32).astype(o_ref.dtype)


def _matmul(a, w, out_dtype=BF16):
    t, k = a.shape
    n = w.shape[1]
    tm, tn = _tile(t, 1024, 8), _tile(n, 512, 128)
    blocks = _nbytes((tm, k), BF16) + _nbytes((k, tn), BF16) + _nbytes((tm, tn), out_dtype)
    return pl.pallas_call(
        _matmul_kernel,
        grid=(t // tm, n // tn),
        in_specs=[pl.BlockSpec((tm, k), lambda i, j: (i, 0)),
                  pl.BlockSpec((k, tn), lambda i, j: (0, j))],
        out_specs=pl.BlockSpec((tm, tn), lambda i, j: (i, j)),
        out_shape=jax.ShapeDtypeStruct((t, n), out_dtype),
        compiler_params=_params(("parallel", "parallel"), blocks, _nbytes((tm, tn), F32)),
        name="projection",
    )(a, w)


def _gate_up_kernel(h_ref, wg_ref, wu_ref, o_ref):
    h = h_ref[...]
    g = jnp.dot(h, wg_ref[...], preferred_element_type=F32)
    u = jnp.dot(h, wu_ref[...], preferred_element_type=F32)
    o_ref[...] = (g * jax.nn.sigmoid(g) * u).astype(o_ref.dtype)


def _gate_up(h, wg, wu):
    t, k = h.shape
    n = wg.shape[1]
    tm, tn = _tile(t, 1024, 8), _tile(n, 512, 128)
    blocks = _nbytes((tm, k), BF16) + 2 * _nbytes((k, tn), BF16) + _nbytes((tm, tn), BF16)
    return pl.pallas_call(
        _gate_up_kernel,
        grid=(t // tm, n // tn),
        in_specs=[pl.BlockSpec((tm, k), lambda i, j: (i, 0)),
                  pl.BlockSpec((k, tn), lambda i, j: (0, j)),
                  pl.BlockSpec((k, tn), lambda i, j: (0, j))],
        out_specs=pl.BlockSpec((tm, tn), lambda i, j: (i, j)),
        out_shape=jax.ShapeDtypeStruct((t, n), BF16),
        compiler_params=_params(("parallel", "parallel"), blocks, 3 * _nbytes((tm, tn), F32)),
        name="ffn_gate_up",
    )(h, wg, wu)


def _residual_kernel(*refs, n_parts, coef):
    a_refs, w_refs = refs[:n_parts], refs[n_parts:2 * n_parts]
    x_ref, gate_ref, o_ref = refs[2 * n_parts:]
    acc = jnp.dot(a_refs[0][...], w_refs[0][...], preferred_element_type=F32)
    for a_ref, w_ref in zip(a_refs[1:], w_refs[1:]):
        acc += jnp.dot(a_ref[...], w_ref[...], preferred_element_type=F32)
    o_ref[...] = x_ref[...] + (coef * gate_ref[0]) * acc


def _residual_matmul(parts, w, x, gate, mod_rows, coef):
    t, n = x.shape
    widths = [p.shape[1] for p in parts]
    assert len(set(widths)) == 1 and sum(widths) == w.shape[0]
    kp = widths[0]
    tm, tn = _tile(mod_rows.span, 512, 8), _tile(n, 512, 128)
    blocks = (len(parts) * (_nbytes((tm, kp), BF16) + _nbytes((kp, tn), BF16))
              + 2 * _nbytes((tm, tn), F32) + _nbytes((1, tn), F32))
    in_specs = [pl.BlockSpec((tm, kp), lambda i, j: (i, 0)) for _ in parts]
    in_specs += [pl.BlockSpec((kp, tn), functools.partial(lambda i, j, p: (p, j), p=p)) for p in range(len(parts))]
    in_specs += [pl.BlockSpec((tm, tn), lambda i, j: (i, j)),
                 pl.BlockSpec((1, 1, tn), lambda i, j: (mod_rows.of_tile(i, tm), 0, j))]
    return pl.pallas_call(
        functools.partial(_residual_kernel, n_parts=len(parts), coef=coef),
        grid=(t // tm, n // tn),
        in_specs=in_specs,
        out_specs=pl.BlockSpec((tm, tn), lambda i, j: (i, j)),
        out_shape=jax.ShapeDtypeStruct((t, n), F32),
        compiler_params=_params(("parallel", "parallel"), blocks, _nbytes((tm, tn), F32)),
        name="residual_matmul",
    )(*parts, *([w] * len(parts)), x, gate)


def _gelu_tanh(x):
    return x * (0.5 * (1.0 + jnp.tanh(math.sqrt(2.0 / math.pi) * (x + 0.044715 * (x * x * x)))))


def _gmlp_kernel(uv_ref, ws_ref, bs_ref, vg_ref, o_ref, *, n_groups, n_chunks):
    for g in range(n_groups):
        u_cols = slice(g * HEAD_DIM, (g + 1) * HEAD_DIM)
        v_cols = slice((n_groups + g) * HEAD_DIM, (n_groups + g + 1) * HEAD_DIM)
        w = ws_ref[g]
        b = bs_ref[g]
        vg = vg_ref[:, u_cols]

        def chunk(c, carry, u_cols=u_cols, v_cols=v_cols, w=w, b=b, vg=vg):
            rows = pl.ds(pl.multiple_of(c * CHUNK, CHUNK), CHUNK)
            u = _gelu_tanh(uv_ref[rows, u_cols].astype(F32))
            v = _gelu_tanh(uv_ref[rows, v_cols].astype(F32))
            v = v - jnp.mean(v, axis=-1, keepdims=True)
            vn = (v * lax.rsqrt(jnp.mean(v * v, axis=-1, keepdims=True) + NORM_EPS)) * vg
            s = jnp.dot(w, vn.astype(BF16), preferred_element_type=F32) + b
            o_ref[rows, u_cols] = (u * s).astype(o_ref.dtype)
            return carry

        lax.fori_loop(0, n_chunks, chunk, 0)


def _gmlp(proj, w_s, b_s, vnorm_g):
    t = proj.shape[0]
    n_groups = w_s.shape[0]
    gm = n_groups * HEAD_DIM
    tc = _tile(t, 512, CHUNK)
    blocks = (_nbytes((tc, 2 * gm), BF16) + _nbytes((tc, gm), BF16) + _nbytes(w_s.shape, BF16)
              + _nbytes((n_groups, CHUNK, 128), F32) + _nbytes((8, gm), F32))
    return pl.pallas_call(
        functools.partial(_gmlp_kernel, n_groups=n_groups, n_chunks=tc // CHUNK),
        grid=(t // tc,),
        in_specs=[pl.BlockSpec((tc, 2 * gm), lambda i: (i, 0)),
                  pl.BlockSpec(w_s.shape, lambda i: (0, 0, 0)),
                  pl.BlockSpec((n_groups, CHUNK, 1), lambda i: (0, 0, 0)),
                  pl.BlockSpec((1, gm), lambda i: (0, 0))],
        out_specs=pl.BlockSpec((tc, gm), lambda i: (i, 0)),
        out_shape=jax.ShapeDtypeStruct((t, gm), BF16),
        compiler_params=_params(("parallel",), blocks, 16 * _nbytes((CHUNK, HEAD_DIM), F32)),
        name="gmlp_chunk_gating",
    )(proj, w_s.astype(BF16), b_s.reshape(n_groups, CHUNK, 1), vnorm_g.reshape(1, gm))


def _swap_rope_halves(x):
    lane = lax.broadcasted_iota(jnp.int32, x.shape, x.ndim - 1)
    return jnp.where(lane % 64 < 32, pltpu.roll(x, HEAD_DIM - 32, x.ndim - 1), pltpu.roll(x, 32, x.ndim - 1))


def _head_norm_kernel(x_ref, g_ref, *rest, rope):
    o_ref = rest[-1]
    x = x_ref[...].astype(F32)
    y = (x * lax.rsqrt(jnp.mean(x * x, axis=-1, keepdims=True) + NORM_EPS)) * g_ref[0]
    if rope:
        cos_ref, sin_ref = rest[:2]
        y = y * cos_ref[...] + _swap_rope_halves(y) * sin_ref[...]
    o_ref[...] = y.astype(o_ref.dtype)


def _head_norm(src, first_head, gains, rope_tables=None):
    t = src.shape[0]
    n_heads = gains.shape[0]
    seq = rope_tables[0].shape[0] if rope_tables is not None else t
    tm = _tile(seq, 1024, 16)
    per_seq = seq // tm
    in_specs = [pl.BlockSpec((tm, HEAD_DIM), lambda i, h: (i, first_head + h)),
                pl.BlockSpec((1, 1, HEAD_DIM), lambda i, h: (h, 0, 0))]
    args = [src, gains]
    if rope_tables is not None:
        in_specs += [pl.BlockSpec((tm, HEAD_DIM), lambda i, h: (i % per_seq, 0))] * 2
        args += list(rope_tables)
    blocks = 2 * _nbytes((tm, HEAD_DIM), BF16) + 2 * _nbytes((tm, HEAD_DIM), F32)
    return pl.pallas_call(
        functools.partial(_head_norm_kernel, rope=rope_tables is not None),
        grid=(t // tm, n_heads),
        in_specs=in_specs,
        out_specs=pl.BlockSpec((tm, HEAD_DIM), lambda i, h: (i, h)),
        out_shape=jax.ShapeDtypeStruct((t, n_heads * HEAD_DIM), BF16),
        compiler_params=_params(("parallel", "parallel"), blocks, 6 * _nbytes((tm, HEAD_DIM), F32)),
        name="head_norm_rope" if rope_tables is not None else "head_norm",
    )(*args)


def _rope_tables(seq):
    half = HEAD_DIM // 2
    t = jnp.arange(seq)
    rows = (t // GRID_W).astype(F32)
    cols = (t % GRID_W).astype(F32)
    inv = 1.0 / (ROPE_THETA ** (jnp.arange(0, half, 2, dtype=F32) / half))
    ang = jnp.concatenate([rows[:, None] * inv, rows[:, None] * inv, cols[:, None] * inv, cols[:, None] * inv], -1)
    sign = jnp.where(jnp.arange(HEAD_DIM) % half < half // 2, -1.0, 1.0).astype(F32)
    return jnp.cos(ang), jnp.sin(ang) * sign


def _nt_dot(a, b):
    return lax.dot_general(a, b, (((1,), (1,)), ((), ())), preferred_element_type=F32)


def _sink_column(sink_ref, first, group, rows_per_head):
    head = lax.broadcasted_iota(jnp.int32, (group * rows_per_head, 1), 0) // rows_per_head
    col = jnp.zeros((group * rows_per_head, 1), F32)
    for g in range(group):
        col = jnp.where(head == g, sink_ref[first + g], col)
    return col


def _swa_kernel(sink_ref, q_ref, k_ref, v_ref, kc_ref, vc_ref, o_ref, *, tq, seq, group):
    kv_head, q_tile = pl.program_id(1), pl.program_id(2)
    kc, vc = kc_ref[0], vc_ref[0]
    sink = _sink_column(sink_ref, kv_head * group, group, BLOCK)
    q_off = lax.broadcasted_iota(jnp.int32, (group * BLOCK, 1), 0) % BLOCK
    k_off = lax.broadcasted_iota(jnp.int32, (1, 3 * BLOCK), 1)

    def block(j, carry):
        base = q_tile * tq + j * BLOCK
        start = pl.multiple_of(jnp.clip(base - BLOCK, 0, seq - 3 * BLOCK), BLOCK)
        q_rows = pl.ds(pl.multiple_of(j * BLOCK, BLOCK), BLOCK)
        q = jnp.concatenate([q_ref[0, q_rows, g * HEAD_DIM:(g + 1) * HEAD_DIM] for g in range(group)], axis=0)
        kw = k_ref[0, pl.ds(start, 3 * BLOCK), :]
        vw = v_ref[0, pl.ds(start, 3 * BLOCK), :]
        s_w = _nt_dot(q, kw) * ATTN_SCALE
        s_w = jnp.where(jnp.abs((start + k_off) - (base + q_off)) <= WINDOW, s_w, NEG_INF)
        s_c = _nt_dot(q, kc) * ATTN_SCALE
        m = jnp.maximum(jnp.maximum(jnp.max(s_w, -1, keepdims=True), jnp.max(s_c, -1, keepdims=True)), sink)
        p_w = jnp.exp(s_w - m)
        p_c = jnp.exp(s_c - m)
        den = jnp.sum(p_w, -1, keepdims=True) + jnp.sum(p_c, -1, keepdims=True) + jnp.exp(sink - m)
        o = (jnp.dot(p_w.astype(BF16), vw, preferred_element_type=F32)
             + jnp.dot(p_c.astype(BF16), vc, preferred_element_type=F32)) / den
        for g in range(group):
            o_ref[0, q_rows, g * HEAD_DIM:(g + 1) * HEAD_DIM] = o[g * BLOCK:(g + 1) * BLOCK].astype(o_ref.dtype)
        return carry

    lax.fori_loop(0, tq // BLOCK, block, 0)


def _swa(qk, proj, kc, projc, sink, *, n_heads, v_col, batch, seq):
    group = n_heads // SWA_KV_HEADS
    lc = kc.shape[0] // batch
    gw = group * HEAD_DIM
    tq = _tile(seq, 1024, BLOCK)
    qk3 = qk.reshape(batch, seq, qk.shape[1])
    proj3 = proj.reshape(batch, seq, proj.shape[1])
    kc3 = kc.reshape(batch, lc, kc.shape[1])
    projc3 = projc.reshape(batch, lc, projc.shape[1])
    blocks = (2 * _nbytes((tq, gw), BF16) + 2 * _nbytes((seq, HEAD_DIM), BF16) + 2 * _nbytes((lc, HEAD_DIM), BF16))
    temps = 6 * _nbytes((group * BLOCK, 3 * BLOCK + lc), F32)
    out = pl.pallas_call(
        functools.partial(_swa_kernel, tq=tq, seq=seq, group=group),
        grid=(batch, SWA_KV_HEADS, seq // tq),
        in_specs=[pl.BlockSpec(memory_space=pltpu.SMEM),
                  pl.BlockSpec((1, tq, gw), lambda b, h, i: (b, i, h)),
                  pl.BlockSpec((1, seq, HEAD_DIM), lambda b, h, i: (b, 0, n_heads + h)),
                  pl.BlockSpec((1, seq, HEAD_DIM), lambda b, h, i: (b, 0, v_col + h)),
                  pl.BlockSpec((1, lc, HEAD_DIM), lambda b, h, i: (b, 0, h)),
                  pl.BlockSpec((1, lc, HEAD_DIM), lambda b, h, i: (b, 0, v_col + h))],
        out_specs=pl.BlockSpec((1, tq, gw), lambda b, h, i: (b, i, h)),
        out_shape=jax.ShapeDtypeStruct((batch, seq, n_heads * HEAD_DIM), BF16),
        compiler_params=_params(("parallel", "parallel", "parallel"), blocks, temps),
        name="window_attention",
    )(sink, qk3, qk3, proj3, kc3, projc3)
    return out.reshape(batch * seq, n_heads * HEAD_DIM)


def _ctx_attn_kernel(*refs, group, lc, use_sink):
    if use_sink:
        sink_ref, q_ref, k_ref, v_ref, o_ref = refs
    else:
        q_ref, k_ref, v_ref, o_ref = refs
    q = jnp.concatenate([q_ref[0, :, g * HEAD_DIM:(g + 1) * HEAD_DIM] for g in range(group)], axis=0)
    s = _nt_dot(q, k_ref[0]) * ATTN_SCALE
    m = jnp.max(s, -1, keepdims=True)
    if use_sink:
        sink = _sink_column(sink_ref, pl.program_id(1) * group, group, lc)
        m = jnp.maximum(m, sink)
    p = jnp.exp(s - m)
    den = jnp.sum(p, -1, keepdims=True)
    if use_sink:
        den = den + jnp.exp(sink - m)
    o = jnp.dot(p.astype(BF16), v_ref[0], preferred_element_type=F32) / den
    for g in range(group):
        o_ref[0, :, g * HEAD_DIM:(g + 1) * HEAD_DIM] = o[g * lc:(g + 1) * lc].astype(o_ref.dtype)


def _ctx_attn(qc, kc, projc, sink, *, n_heads, n_kv, v_col, batch):
    group = n_heads // n_kv
    lc = qc.shape[0] // batch
    gw = group * HEAD_DIM
    qc3 = qc.reshape(batch, lc, qc.shape[1])
    kc3 = kc.reshape(batch, lc, kc.shape[1])
    projc3 = projc.reshape(batch, lc, projc.shape[1])
    use_sink = sink is not None
    in_specs = [pl.BlockSpec((1, lc, gw), lambda b, h: (b, 0, h)),
                pl.BlockSpec((1, lc, HEAD_DIM), lambda b, h: (b, 0, h)),
                pl.BlockSpec((1, lc, HEAD_DIM), lambda b, h: (b, 0, v_col + h))]
    args = [qc3, kc3, projc3]
    if use_sink:
        in_specs = [pl.BlockSpec(memory_space=pltpu.SMEM)] + in_specs
        args = [sink] + args
    blocks = 2 * _nbytes((lc, gw), BF16) + 2 * _nbytes((lc, HEAD_DIM), BF16)
    out = pl.pallas_call(
        functools.partial(_ctx_attn_kernel, group=group, lc=lc, use_sink=use_sink),
        grid=(batch, n_kv),
        in_specs=in_specs,
        out_specs=pl.BlockSpec((1, lc, gw), lambda b, h: (b, 0, h)),
        out_shape=jax.ShapeDtypeStruct((batch, lc, n_heads * HEAD_DIM), BF16),
        compiler_params=_params(("parallel", "parallel"), blocks, 6 * _nbytes((group * lc, lc), F32)),
        name="context_attention",
    )(*args)
    return out.reshape(batch * lc, n_heads * HEAD_DIM)


def _nat_bias_kernel(rpb_ref, o_ref):
    h = pl.program_id(0)
    n_dc = 2 * NA_KW - 1
    w = lax.broadcasted_iota(jnp.int32, (GRID_W, 2 * GRID_W), 0)
    lane = lax.broadcasted_iota(jnp.int32, (GRID_W, 2 * GRID_W), 1)
    second = lane >= GRID_W
    kc = lane % GRID_W
    c0 = jnp.clip(w - NA_KW // 2, 0, GRID_W - NA_KW)
    dc = jnp.where((kc >= c0) & (kc < c0 + NA_KW), kc - w + (NA_KW - 1), -1)

    def pair(dr, carry):
        base = (h * (2 * NA_KH - 1) + dr) * n_dc
        tile = jnp.full((GRID_W, 2 * GRID_W), NEG_INF, F32)
        for j in range(n_dc):
            tile = jnp.where(dc == j, jnp.where(second, rpb_ref[base + n_dc + j], rpb_ref[base + j]), tile)
        o_ref[0, dr] = tile
        return carry

    lax.fori_loop(0, 2 * NA_KH - 2, pair, 0)


def _nat_bias(rpb):
    n_heads = rpb.shape[0]
    assert rpb.shape[1:] == (2 * NA_KH - 1, 2 * NA_KW - 1)
    shape = (n_heads, 2 * NA_KH - 2, GRID_W, 2 * GRID_W)
    return pl.pallas_call(
        _nat_bias_kernel,
        grid=(n_heads,),
        in_specs=[pl.BlockSpec(memory_space=pltpu.SMEM)],
        out_specs=pl.BlockSpec((1,) + shape[1:], lambda h: (h, 0, 0, 0)),
        out_shape=jax.ShapeDtypeStruct(shape, F32),
        compiler_params=_params(("parallel",), _nbytes(shape[1:], F32)),
        name="nat_bias_tiles",
    )(rpb.reshape(-1))


def _nat_kernel(q_ref, k_ref, v_ref, kc_ref, vc_ref, bias_ref, o_ref, *, rows):
    kc, vc = kc_ref[0], vc_ref[0]
    n_pairs = NA_KH // 2

    def row(r, carry):
        r0 = jnp.clip(r - NA_KH // 2, 0, rows - NA_KH)
        dr0 = r0 - r + (NA_KH - 1)
        q = q_ref[0, pl.ds(pl.multiple_of(r * GRID_W, GRID_W), GRID_W), :]
        k_rows = pl.ds(pl.multiple_of(r0 * GRID_W, GRID_W), NA_KH * GRID_W)
        s_nb = _nt_dot(q, k_ref[0, k_rows, :]) * ATTN_SCALE
        s = [s_nb[:, 2 * GRID_W * j:2 * GRID_W * (j + 1)] + bias_ref[0, dr0 + 2 * j] for j in range(n_pairs)]
        s_c = _nt_dot(q, kc) * ATTN_SCALE
        m = jnp.max(s_c, -1, keepdims=True)
        for sj in s:
            m = jnp.maximum(m, jnp.max(sj, -1, keepdims=True))
        p = [jnp.exp(sj - m) for sj in s]
        p_c = jnp.exp(s_c - m)
        den = jnp.sum(p_c, -1, keepdims=True)
        for pj in p:
            den = den + jnp.sum(pj, -1, keepdims=True)
        p_nb = jnp.concatenate(p, axis=-1).astype(BF16)
        o = (jnp.dot(p_nb, v_ref[0, k_rows, :], preferred_element_type=F32)
             + jnp.dot(p_c.astype(BF16), vc, preferred_element_type=F32)) / den
        o_ref[0, pl.ds(pl.multiple_of(r * GRID_W, GRID_W), GRID_W), :] = o.astype(o_ref.dtype)
        return carry

    lax.fori_loop(0, rows, row, 0)


def _nat(qk, proj, kc, projc, bias, *, n_heads, v_col, batch, seq):
    rows = seq // GRID_W
    assert rows >= NA_KH and seq % GRID_W == 0
    lc = kc.shape[0] // batch
    qk3 = qk.reshape(batch, seq, qk.shape[1])
    proj3 = proj.reshape(batch, seq, proj.shape[1])
    kc3 = kc.reshape(batch, lc, kc.shape[1])
    projc3 = projc.reshape(batch, lc, projc.shape[1])
    head_block = pl.BlockSpec((1, seq, HEAD_DIM), lambda b, h: (b, 0, h))
    blocks = (4 * _nbytes((seq, HEAD_DIM), BF16) + 2 * _nbytes((lc, HEAD_DIM), BF16) + _nbytes(bias.shape[1:], F32))
    out = pl.pallas_call(
        functools.partial(_nat_kernel, rows=rows),
        grid=(batch, n_heads),
        in_specs=[head_block,
                  pl.BlockSpec((1, seq, HEAD_DIM), lambda b, h: (b, 0, n_heads + h)),
                  pl.BlockSpec((1, seq, HEAD_DIM), lambda b, h: (b, 0, v_col + h)),
                  pl.BlockSpec((1, lc, HEAD_DIM), lambda b, h: (b, 0, h)),
                  pl.BlockSpec((1, lc, HEAD_DIM), lambda b, h: (b, 0, v_col + h)),
                  pl.BlockSpec((1,) + bias.shape[1:], lambda b, h: (h, 0, 0, 0))],
        out_specs=head_block,
        out_shape=jax.ShapeDtypeStruct((batch, seq, n_heads * HEAD_DIM), BF16),
        compiler_params=_params(("parallel", "parallel"), blocks, 8 * _nbytes((GRID_W, NA_KH * GRID_W + lc), F32)),
        name="neighbourhood_attention",
    )(qk3, qk3, proj3, kc3, projc3, bias)
    return out.reshape(batch * seq, n_heads * HEAD_DIM)


def _gains(*groups):
    return jnp.concatenate([jnp.broadcast_to(g.astype(F32), (n, HEAD_DIM)) for g, n in groups]).reshape(-1, 1, HEAD_DIM)


def kernel(x, c, ctx, c_ctx, w_ada, b_ada, norm_g, ffn_w_gate, ffn_w_up, ffn_w_down, mix_ab_w_in, mix_ab_w_out,
           gm_vnorm_g, gm_w_s, gm_b_s, swa_qnorm_g, swa_knorm_g, swa_sink, nat_w_qkv, nat_w_out, nat_qnorm_g,
           nat_knorm_g, nat_rpb):
    batch, seq, d = x.shape
    lc = ctx.shape[1]
    depth = w_ada.shape[0]
    ctx_row = batch
    assert batch + 1 <= MOD_ROWS

    cond = jnp.zeros((MOD_ROWS, d), F32).at[:batch].set(c).at[ctx_row].set(c_ctx)
    mod = _ada(cond, w_ada, b_ada).reshape(depth, MOD_ROWS, N_MOD, 1, d)

    lat_row = _ModRows(0, seq)
    ctx_row_of = _ModRows(ctx_row, batch * lc)

    def mod_vec(layer, sub, which):
        return mod[layer, :, 3 * sub + which]

    def ffn(xs, row, layer, sub, weights):
        wg, wu, wd = weights
        h = _norm_mod(xs, norm_g[layer, sub], mod_vec(layer, sub, 0), mod_vec(layer, sub, 1), row)
        return _residual_matmul([_gate_up(h, wg, wu)], wd, xs, mod_vec(layer, sub, 2), row, HALF_STEP)

    xs = x.reshape(batch * seq, d)
    cs = ctx.reshape(batch * lc, d)
    rope = _rope_tables(seq)

    for layer in range(depth):
        last = layer == depth - 1
        j = layer // 2
        ffn_w = [tuple(w[layer, idx].astype(BF16) for w in (ffn_w_gate, ffn_w_up, ffn_w_down)) for idx in range(2)]
        xs = ffn(xs, lat_row, layer, 0, ffn_w[0])
        cs = ffn(cs, ctx_row_of, layer, 0, ffn_w[0])
        h = _norm_mod(xs, norm_g[layer, 1], mod_vec(layer, 1, 0), mod_vec(layer, 1, 1), lat_row)
        hc = _norm_mod(cs, norm_g[layer, 1], mod_vec(layer, 1, 0), mod_vec(layer, 1, 1), ctx_row_of)
        gate = mod_vec(layer, 1, 2)
        if layer % 2 == 0:
            n_groups = gm_w_s.shape[1]
            gm = n_groups * HEAD_DIM
            n_heads = swa_sink.shape[1]
            q_col = 2 * n_groups
            v_col = q_col + n_heads + SWA_KV_HEADS
            w_in = mix_ab_w_in[j].astype(BF16)
            w_out = mix_ab_w_out[j].astype(BF16)
            proj = _matmul(h, w_in)
            projc = _matmul(hc, w_in)
            qk_gain = _gains((swa_qnorm_g[j], n_heads), (swa_knorm_g[j], SWA_KV_HEADS))
            qk = _head_norm(proj, q_col, qk_gain, rope)
            qkc = _head_norm(projc, q_col, qk_gain)
            kc = qkc[:, n_heads * HEAD_DIM:]
            y_g = _gmlp(proj, gm_w_s[j], gm_b_s[j], gm_vnorm_g[j])
            y_a = _swa(qk, proj, kc, projc, swa_sink[j], n_heads=n_heads, v_col=v_col, batch=batch, seq=seq)
            assert gm == n_heads * HEAD_DIM
            xs = _residual_matmul([y_g, y_a], w_out, xs, gate, lat_row, 1.0)
            if not last:
                yc_g = _gmlp(projc, gm_w_s[j], gm_b_s[j], gm_vnorm_g[j])
                yc_a = _ctx_attn(qkc, kc, projc, swa_sink[j], n_heads=n_heads, n_kv=SWA_KV_HEADS, v_col=v_col,
                                 batch=batch)
                cs = _residual_matmul([yc_g, yc_a], w_out, cs, gate, ctx_row_of, 1.0)
        else:
            n_heads = nat_rpb.shape[1]
            v_col = 2 * n_heads
            w_qkv = nat_w_qkv[j].astype(BF16)
            w_out = nat_w_out[j].astype(BF16)
            proj = _matmul(h, w_qkv)
            projc = _matmul(hc, w_qkv)
            qk_gain = _gains((nat_qnorm_g[j], n_heads), (nat_knorm_g[j], n_heads))
            qk = _head_norm(proj, 0, qk_gain)
            qkc = _head_norm(projc, 0, qk_gain)
            kc = qkc[:, n_heads * HEAD_DIM:]
            y = _nat(qk, proj, kc, projc, _nat_bias(nat_rpb[j]), n_heads=n_heads, v_col=v_col, batch=batch, seq=seq)
            xs = _residual_matmul([y], w_out, xs, gate, lat_row, 1.0)
            if not last:
                yc = _ctx_attn(qkc, kc, projc, None, n_heads=n_heads, n_kv=n_heads, v_col=v_col, batch=batch)
                cs = _residual_matmul([yc], w_out, cs, gate, ctx_row_of, 1.0)
        xs = ffn(xs, lat_row, layer, 2, ffn_w[1])
        if not last:
            cs = ffn(cs, ctx_row_of, layer, 2, ffn_w[1])
    return xs.reshape(batch, seq, d)
```

```python
import functools
import math

import jax
import jax.numpy as jnp
from jax import lax
from jax.experimental import pallas as pl
from jax.experimental.pallas import tpu as pltpu

F32 = jnp.float32
BF16 = jnp.bfloat16

HEAD_DIM = 128
GRID_W = 64
CHUNK = 128
WINDOW = 128
BLOCK = 128
SWA_KV_HEADS = 4
NA_KH = 8
NA_KW = 16
NAT_ROW_UNROLL = 8
SWA_BLOCK_UNROLL = 2
HEAD_NORM_GROUP = 4
N_MOD = 9
HALF_STEP = 0.5
ROPE_THETA = 10000.0
NORM_EPS = 1e-6
NEG_INF = -1e30
ATTN_SCALE = HEAD_DIM ** -0.5

V7X_VMEM_BYTES = 64 * 1024 * 1024
VMEM_CEILING = V7X_VMEM_BYTES - 8 * 1024 * 1024
MOD_ROWS = 8
LHS_BLOCK_BUDGET = 8 * 1024 * 1024


def _vmem_limit(pipelined_bytes, temp_bytes=0):
    need = int((2 * pipelined_bytes + temp_bytes) * 1.25) + (2 << 20)
    return min(max(need, 16 << 20), VMEM_CEILING)


def _params(semantics, pipelined_bytes, temp_bytes=0):
    return pltpu.CompilerParams(dimension_semantics=semantics,
                                vmem_limit_bytes=_vmem_limit(pipelined_bytes, temp_bytes))


def _tile(n, pref, quantum):
    if n <= pref:
        return n
    t = (pref // quantum) * quantum
    while t > quantum and n % t:
        t -= quantum
    assert n % t == 0, (n, pref, quantum)
    return t


def _nbytes(shape, dtype):
    return math.prod(shape) * jnp.dtype(dtype).itemsize


def _ada_kernel(c_ref, w_ref, b_ref, o_ref):
    c = c_ref[...]
    s = (c * jax.nn.sigmoid(c)).astype(BF16)
    o_ref[0] = jnp.dot(s, w_ref[0].astype(BF16), preferred_element_type=F32) + b_ref[0]


def _ada(cond, w_ada, b_ada):
    depth, d, n = w_ada.shape
    tn = _tile(n, 512, 128)
    blocks = _nbytes((MOD_ROWS, d), F32) + _nbytes((d, tn), F32) + 2 * _nbytes((MOD_ROWS, tn), F32)
    return pl.pallas_call(
        _ada_kernel,
        grid=(depth, n // tn),
        in_specs=[pl.BlockSpec((MOD_ROWS, d), lambda l, j: (0, 0)),
                  pl.BlockSpec((1, d, tn), lambda l, j: (l, 0, j)),
                  pl.BlockSpec((1, 1, tn), lambda l, j: (l, 0, j))],
        out_specs=pl.BlockSpec((1, MOD_ROWS, tn), lambda l, j: (l, 0, j)),
        out_shape=jax.ShapeDtypeStruct((depth, MOD_ROWS, n), F32),
        compiler_params=_params(("parallel", "parallel"), blocks, _nbytes((d, tn), BF16)),
        name="ada_modulation",
    )(cond, w_ada, b_ada.reshape(depth, 1, n))


def _norm_mod_kernel(x_ref, g_ref, shift_ref, scale_ref, o_ref):
    x = x_ref[...]
    y = x * lax.rsqrt(jnp.mean(x * x, axis=-1, keepdims=True) + NORM_EPS)
    o_ref[...] = ((y * g_ref[...]) * (1.0 + scale_ref[0]) + shift_ref[0]).astype(o_ref.dtype)


class _ModRows:
    def __init__(self, first, span):
        self.first, self.span = first, span

    def of_tile(self, i, tm):
        assert self.span % tm == 0
        return self.first + (i * tm) // self.span


def _norm_mod(x, g, shift, scale, mod_rows):
    t, d = x.shape
    tm = _tile(mod_rows.span, 256, 8)
    mod_spec = pl.BlockSpec((1, 1, d), lambda i: (mod_rows.of_tile(i, tm), 0, 0))
    blocks = _nbytes((tm, d), F32) + _nbytes((tm, d), BF16) + 3 * _nbytes((1, d), F32)
    return pl.pallas_call(
        _norm_mod_kernel,
        grid=(t // tm,),
        in_specs=[pl.BlockSpec((tm, d), lambda i: (i, 0)),
                  pl.BlockSpec((1, d), lambda i: (0, 0)),
                  mod_spec, mod_spec],
        out_specs=pl.BlockSpec((tm, d), lambda i: (i, 0)),
        out_shape=jax.ShapeDtypeStruct((t, d), BF16),
        compiler_params=_params(("parallel",), blocks, 2 * _nbytes((tm, d), F32)),
        name="norm_modulate",
    )(x, g.reshape(1, d), shift, scale)


def _weight_spec(lead, k, tn, k_block=0):
    return pl.BlockSpec((None,) * len(lead) + (k, tn), lambda i, j: lead + (k_block, j))


def _matmul_kernel(a_ref, w_ref, o_ref):
    o_ref[...] = jnp.dot(a_ref[...], w_ref[...], preferred_element_type=F32).astype(o_ref.dtype)


def _matmul(a, w, lead, out_dtype=BF16):
    t, k = a.shape
    n = w.shape[-1]
    tm, tn = _tile(t, 1024, 8), _tile(n, 512, 128)
    blocks = _nbytes((tm, k), BF16) + _nbytes((k, tn), BF16) + _nbytes((tm, tn), out_dtype)
    return pl.pallas_call(
        _matmul_kernel,
        grid=(t // tm, n // tn),
        in_specs=[pl.BlockSpec((tm, k), lambda i, j: (i, 0)), _weight_spec(lead, k, tn)],
        out_specs=pl.BlockSpec((tm, tn), lambda i, j: (i, j)),
        out_shape=jax.ShapeDtypeStruct((t, n), out_dtype),
        compiler_params=_params(("parallel", "parallel"), blocks, _nbytes((tm, tn), F32)),
        name="projection",
    )(a, w)


def _gate_up_kernel(h_ref, wg_ref, wu_ref, o_ref):
    h = h_ref[...]
    g = jnp.dot(h, wg_ref[...], preferred_element_type=F32)
    u = jnp.dot(h, wu_ref[...], preferred_element_type=F32)
    o_ref[...] = (g * jax.nn.sigmoid(g) * u).astype(o_ref.dtype)


def _gate_up(h, wg, wu, lead):
    t, k = h.shape
    n = wg.shape[-1]
    tm, tn = _tile(t, 1024, 8), _tile(n, 512, 128)
    blocks = _nbytes((tm, k), BF16) + 2 * _nbytes((k, tn), BF16) + _nbytes((tm, tn), BF16)
    return pl.pallas_call(
        _gate_up_kernel,
        grid=(t // tm, n // tn),
        in_specs=[pl.BlockSpec((tm, k), lambda i, j: (i, 0)), _weight_spec(lead, k, tn), _weight_spec(lead, k, tn)],
        out_specs=pl.BlockSpec((tm, tn), lambda i, j: (i, j)),
        out_shape=jax.ShapeDtypeStruct((t, n), BF16),
        compiler_params=_params(("parallel", "parallel"), blocks, 3 * _nbytes((tm, tn), F32)),
        name="ffn_gate_up",
    )(h, wg, wu)


def _residual_kernel(*refs, n_parts, coef):
    a_refs, w_refs = refs[:n_parts], refs[n_parts:2 * n_parts]
    x_ref, gate_ref, o_ref = refs[2 * n_parts:]
    acc = jnp.dot(a_refs[0][...], w_refs[0][...], preferred_element_type=F32)
    for a_ref, w_ref in zip(a_refs[1:], w_refs[1:]):
        acc += jnp.dot(a_ref[...], w_ref[...], preferred_element_type=F32)
    o_ref[...] = x_ref[...] + (coef * gate_ref[0]) * acc


def _residual_matmul(parts, w, lead, x, gate, mod_rows, coef):
    t, n = x.shape
    widths = [p.shape[1] for p in parts]
    assert len(set(widths)) == 1 and sum(widths) == w.shape[-2]
    kp = widths[0]
    tm = _tile(mod_rows.span, 1024, 8)
    tn = _tile(n, 512 if _nbytes((tm, sum(widths)), BF16) <= LHS_BLOCK_BUDGET else 256, 128)
    blocks = (len(parts) * (_nbytes((tm, kp), BF16) + _nbytes((kp, tn), BF16))
              + 2 * _nbytes((tm, tn), F32) + _nbytes((1, tn), F32))
    in_specs = [pl.BlockSpec((tm, kp), lambda i, j: (i, 0)) for _ in parts]
    in_specs += [_weight_spec(lead, kp, tn, k_block=p) for p in range(len(parts))]
    in_specs += [pl.BlockSpec((tm, tn), lambda i, j: (i, j)),
                 pl.BlockSpec((1, 1, tn), lambda i, j: (mod_rows.of_tile(i, tm), 0, j))]
    return pl.pallas_call(
        functools.partial(_residual_kernel, n_parts=len(parts), coef=coef),
        grid=(t // tm, n // tn),
        in_specs=in_specs,
        out_specs=pl.BlockSpec((tm, tn), lambda i, j: (i, j)),
        out_shape=jax.ShapeDtypeStruct((t, n), F32),
        compiler_params=_params(("parallel", "parallel"), blocks, _nbytes((tm, tn), F32)),
        name="residual_matmul",
    )(*parts, *([w] * len(parts)), x, gate)


def _gelu_tanh(x):
    return x * (0.5 * (1.0 + jnp.tanh(math.sqrt(2.0 / math.pi) * (x + 0.044715 * (x * x * x)))))


def _gmlp_kernel(uv_ref, ws_ref, bs_ref, vg_ref, o_ref, *, n_groups, n_chunks):
    for g in range(n_groups):
        u_cols = slice(g * HEAD_DIM, (g + 1) * HEAD_DIM)
        v_cols = slice((n_groups + g) * HEAD_DIM, (n_groups + g + 1) * HEAD_DIM)
        w = ws_ref[g]
        b = bs_ref[g]
        vg = vg_ref[:, u_cols]

        for c in range(n_chunks):
            rows = slice(c * CHUNK, (c + 1) * CHUNK)
            u = _gelu_tanh(uv_ref[rows, u_cols].astype(F32))
            v = _gelu_tanh(uv_ref[rows, v_cols].astype(F32))
            v = v - jnp.mean(v, axis=-1, keepdims=True)
            vn = (v * lax.rsqrt(jnp.mean(v * v, axis=-1, keepdims=True) + NORM_EPS)) * vg
            s = jnp.dot(w, vn.astype(BF16), preferred_element_type=F32) + b
            o_ref[rows, u_cols] = (u * s).astype(o_ref.dtype)


def _gmlp(proj, w_s, b_s, vnorm_g):
    t = proj.shape[0]
    n_groups = w_s.shape[0]
    gm = n_groups * HEAD_DIM
    tc = _tile(t, 512, CHUNK)
    blocks = (_nbytes((tc, 2 * gm), BF16) + _nbytes((tc, gm), BF16) + _nbytes(w_s.shape, BF16)
              + _nbytes((n_groups, CHUNK, 128), F32) + _nbytes((8, gm), F32))
    return pl.pallas_call(
        functools.partial(_gmlp_kernel, n_groups=n_groups, n_chunks=tc // CHUNK),
        grid=(t // tc,),
        in_specs=[pl.BlockSpec((tc, 2 * gm), lambda i: (i, 0)),
                  pl.BlockSpec(w_s.shape, lambda i: (0, 0, 0)),
                  pl.BlockSpec((n_groups, CHUNK, 1), lambda i: (0, 0, 0)),
                  pl.BlockSpec((1, gm), lambda i: (0, 0))],
        out_specs=pl.BlockSpec((tc, gm), lambda i: (i, 0)),
        out_shape=jax.ShapeDtypeStruct((t, gm), BF16),
        compiler_params=_params(("parallel",), blocks, 16 * _nbytes((CHUNK, HEAD_DIM), F32)),
        name="gmlp_chunk_gating",
    )(proj, w_s.astype(BF16), b_s.reshape(n_groups, CHUNK, 1), vnorm_g.reshape(1, gm))


def _swap_rope_halves(x):
    lane = lax.broadcasted_iota(jnp.int32, x.shape, x.ndim - 1)
    return jnp.where(lane % 64 < 32, pltpu.roll(x, HEAD_DIM - 32, x.ndim - 1), pltpu.roll(x, 32, x.ndim - 1))


def _rms_head(x, g):
    x = x.astype(F32)
    return (x * lax.rsqrt(jnp.mean(x * x, axis=-1, keepdims=True) + NORM_EPS)) * g


def _head_norm_kernel(x_ref, g_ref, *rest, rope, heads):
    o_ref = rest[-1]
    for h in range(heads):
        cols = slice(h * HEAD_DIM, (h + 1) * HEAD_DIM)
        y = _rms_head(x_ref[:, cols], g_ref[:, cols])
        if rope:
            cos_ref, sin_ref = rest[:2]
            y = y * cos_ref[...] + _swap_rope_halves(y) * sin_ref[...]
        o_ref[:, cols] = y.astype(o_ref.dtype)


def _head_norm(src, first_head, gains, rope_tables=None):
    t = src.shape[0]
    n_heads = gains.shape[1] // HEAD_DIM
    heads = HEAD_NORM_GROUP if first_head % HEAD_NORM_GROUP == 0 and n_heads % HEAD_NORM_GROUP == 0 else 1
    width = heads * HEAD_DIM
    seq = rope_tables[0].shape[0] if rope_tables is not None else t
    tm = _tile(seq, 1024, 16)
    per_seq = seq // tm
    in_specs = [pl.BlockSpec((tm, width), lambda i, h: (i, first_head // heads + h)),
                pl.BlockSpec((1, width), lambda i, h: (0, h))]
    args = [src, gains]
    if rope_tables is not None:
        in_specs += [pl.BlockSpec((tm, HEAD_DIM), lambda i, h: (i % per_seq, 0))] * 2
        args += list(rope_tables)
    blocks = 2 * _nbytes((tm, width), BF16) + 2 * _nbytes((tm, HEAD_DIM), F32)
    return pl.pallas_call(
        functools.partial(_head_norm_kernel, rope=rope_tables is not None, heads=heads),
        grid=(t // tm, n_heads // heads),
        in_specs=in_specs,
        out_specs=pl.BlockSpec((tm, width), lambda i, h: (i, h)),
        out_shape=jax.ShapeDtypeStruct((t, n_heads * HEAD_DIM), BF16),
        compiler_params=_params(("parallel", "parallel"), blocks, 6 * heads * _nbytes((tm, HEAD_DIM), F32)),
        name="head_norm_rope" if rope_tables is not None else "head_norm",
    )(*args)


def _rope_tables(seq):
    half = HEAD_DIM // 2
    t = jnp.arange(seq)
    rows = (t // GRID_W).astype(F32)
    cols = (t % GRID_W).astype(F32)
    inv = 1.0 / (ROPE_THETA ** (jnp.arange(0, half, 2, dtype=F32) / half))
    ang = jnp.concatenate([rows[:, None] * inv, rows[:, None] * inv, cols[:, None] * inv, cols[:, None] * inv], -1)
    sign = jnp.where(jnp.arange(HEAD_DIM) % half < half // 2, -1.0, 1.0).astype(F32)
    return jnp.cos(ang), jnp.sin(ang) * sign


def _nt_dot(a, b):
    return lax.dot_general(a, b, (((1,), (1,)), ((), ())), preferred_element_type=F32)


def _sink_column(sink_ref, first, group, rows_per_head):
    head = lax.broadcasted_iota(jnp.int32, (group * rows_per_head, 1), 0) // rows_per_head
    col = jnp.zeros((group * rows_per_head, 1), F32)
    for g in range(group):
        col = jnp.where(head == g, sink_ref[first + g], col)
    return col


def _swa_kernel(sink_ref, q_ref, k_ref, v_ref, kc_ref, vc_ref, o_ref, *, tq, seq, group):
    kv_head, q_tile = pl.program_id(1), pl.program_id(2)
    kc, vc = kc_ref[0], vc_ref[0]
    sink = _sink_column(sink_ref, kv_head * group, group, BLOCK)
    q_off = lax.broadcasted_iota(jnp.int32, (group * BLOCK, 1), 0) % BLOCK
    k_off = lax.broadcasted_iota(jnp.int32, (1, 3 * BLOCK), 1)

    def scores(j):
        base = q_tile * tq + j * BLOCK
        start = pl.multiple_of(jnp.clip(base - BLOCK, 0, seq - 3 * BLOCK), BLOCK)
        q_rows = pl.ds(pl.multiple_of(j * BLOCK, BLOCK), BLOCK)
        q = jnp.concatenate([q_ref[0, q_rows, g * HEAD_DIM:(g + 1) * HEAD_DIM] for g in range(group)], axis=0)
        s_w = _nt_dot(q, k_ref[0, pl.ds(start, 3 * BLOCK), :])
        return base, start, q_rows, s_w, _nt_dot(q, kc)

    def attend(base, start, q_rows, s_w, s_c):
        s_w = jnp.where(jnp.abs((start + k_off) - (base + q_off)) <= WINDOW, s_w, NEG_INF)
        m = jnp.maximum(jnp.maximum(jnp.max(s_w, -1, keepdims=True), jnp.max(s_c, -1, keepdims=True)), sink)
        p_w = jnp.exp(s_w - m)
        p_c = jnp.exp(s_c - m)
        den = jnp.sum(p_w, -1, keepdims=True) + jnp.sum(p_c, -1, keepdims=True) + jnp.exp(sink - m)
        o = (jnp.dot(p_w.astype(BF16), v_ref[0, pl.ds(start, 3 * BLOCK), :], preferred_element_type=F32)
             + jnp.dot(p_c.astype(BF16), vc, preferred_element_type=F32)) / den
        for g in range(group):
            o_ref[0, q_rows, g * HEAD_DIM:(g + 1) * HEAD_DIM] = o[g * BLOCK:(g + 1) * BLOCK].astype(o_ref.dtype)

    def block_group(i, carry):
        staged = [scores(i * SWA_BLOCK_UNROLL + u) for u in range(SWA_BLOCK_UNROLL)]
        for block_state in staged:
            attend(*block_state)
        return carry

    assert (tq // BLOCK) % SWA_BLOCK_UNROLL == 0
    lax.fori_loop(0, tq // BLOCK // SWA_BLOCK_UNROLL, block_group, 0)


def _swa(qk, proj, kc, projc, sink, *, n_heads, v_col, batch, seq):
    group = n_heads // SWA_KV_HEADS
    lc = kc.shape[0] // batch
    gw = group * HEAD_DIM
    tq = _tile(seq, 1024, BLOCK)
    qk3 = qk.reshape(batch, seq, qk.shape[1])
    proj3 = proj.reshape(batch, seq, proj.shape[1])
    kc3 = kc.reshape(batch, lc, kc.shape[1])
    projc3 = projc.reshape(batch, lc, projc.shape[1])
    blocks = (2 * _nbytes((tq, gw), BF16) + 2 * _nbytes((seq, HEAD_DIM), BF16) + 2 * _nbytes((lc, HEAD_DIM), BF16))
    temps = 6 * _nbytes((group * BLOCK, 3 * BLOCK + lc), F32)
    out = pl.pallas_call(
        functools.partial(_swa_kernel, tq=tq, seq=seq, group=group),
        grid=(batch, SWA_KV_HEADS, seq // tq),
        in_specs=[pl.BlockSpec(memory_space=pltpu.SMEM),
                  pl.BlockSpec((1, tq, gw), lambda b, h, i: (b, i, h)),
                  pl.BlockSpec((1, seq, HEAD_DIM), lambda b, h, i: (b, 0, n_heads + h)),
                  pl.BlockSpec((1, seq, HEAD_DIM), lambda b, h, i: (b, 0, v_col + h)),
                  pl.BlockSpec((1, lc, HEAD_DIM), lambda b, h, i: (b, 0, h)),
                  pl.BlockSpec((1, lc, HEAD_DIM), lambda b, h, i: (b, 0, v_col + h))],
        out_specs=pl.BlockSpec((1, tq, gw), lambda b, h, i: (b, i, h)),
        out_shape=jax.ShapeDtypeStruct((batch, seq, n_heads * HEAD_DIM), BF16),
        compiler_params=_params(("parallel", "parallel", "parallel"), blocks, temps),
        name="window_attention",
    )(sink, qk3, qk3, proj3, kc3, projc3)
    return out.reshape(batch * seq, n_heads * HEAD_DIM)


def _ctx_attn_kernel(*refs, group, lc, use_sink):
    if use_sink:
        sink_ref, q_ref, k_ref, v_ref, o_ref = refs
    else:
        q_ref, k_ref, v_ref, o_ref = refs
    q = jnp.concatenate([q_ref[0, :, g * HEAD_DIM:(g + 1) * HEAD_DIM] for g in range(group)], axis=0)
    s = _nt_dot(q, k_ref[0])
    m = jnp.max(s, -1, keepdims=True)
    if use_sink:
        sink = _sink_column(sink_ref, pl.program_id(1) * group, group, lc)
        m = jnp.maximum(m, sink)
    p = jnp.exp(s - m)
    den = jnp.sum(p, -1, keepdims=True)
    if use_sink:
        den = den + jnp.exp(sink - m)
    o = jnp.dot(p.astype(BF16), v_ref[0], preferred_element_type=F32) / den
    for g in range(group):
        o_ref[0, :, g * HEAD_DIM:(g + 1) * HEAD_DIM] = o[g * lc:(g + 1) * lc].astype(o_ref.dtype)


def _ctx_attn(qc, kc, projc, sink, *, n_heads, n_kv, v_col, batch):
    group = n_heads // n_kv
    lc = qc.shape[0] // batch
    gw = group * HEAD_DIM
    qc3 = qc.reshape(batch, lc, qc.shape[1])
    kc3 = kc.reshape(batch, lc, kc.shape[1])
    projc3 = projc.reshape(batch, lc, projc.shape[1])
    use_sink = sink is not None
    in_specs = [pl.BlockSpec((1, lc, gw), lambda b, h: (b, 0, h)),
                pl.BlockSpec((1, lc, HEAD_DIM), lambda b, h: (b, 0, h)),
                pl.BlockSpec((1, lc, HEAD_DIM), lambda b, h: (b, 0, v_col + h))]
    args = [qc3, kc3, projc3]
    if use_sink:
        in_specs = [pl.BlockSpec(memory_space=pltpu.SMEM)] + in_specs
        args = [sink] + args
    blocks = 2 * _nbytes((lc, gw), BF16) + 2 * _nbytes((lc, HEAD_DIM), BF16)
    out = pl.pallas_call(
        functools.partial(_ctx_attn_kernel, group=group, lc=lc, use_sink=use_sink),
        grid=(batch, n_kv),
        in_specs=in_specs,
        out_specs=pl.BlockSpec((1, lc, gw), lambda b, h: (b, 0, h)),
        out_shape=jax.ShapeDtypeStruct((batch, lc, n_heads * HEAD_DIM), BF16),
        compiler_params=_params(("parallel", "parallel"), blocks, 6 * _nbytes((group * lc, lc), F32)),
        name="context_attention",
    )(*args)
    return out.reshape(batch * lc, n_heads * HEAD_DIM)


def _nat_bias_kernel(rpb_ref, o_ref):
    h = pl.program_id(0)
    n_dc = 2 * NA_KW - 1
    w = lax.broadcasted_iota(jnp.int32, (GRID_W, 2 * GRID_W), 0)
    lane = lax.broadcasted_iota(jnp.int32, (GRID_W, 2 * GRID_W), 1)
    second = lane >= GRID_W
    kc = lane % GRID_W
    c0 = jnp.clip(w - NA_KW // 2, 0, GRID_W - NA_KW)
    dc = jnp.where((kc >= c0) & (kc < c0 + NA_KW), kc - w + (NA_KW - 1), -1)

    def pair(dr, carry):
        base = (h * (2 * NA_KH - 1) + dr) * n_dc
        tile = jnp.full((GRID_W, 2 * GRID_W), NEG_INF, F32)
        for j in range(n_dc):
            tile = jnp.where(dc == j, jnp.where(second, rpb_ref[base + n_dc + j], rpb_ref[base + j]), tile)
        o_ref[0, dr] = tile
        return carry

    lax.fori_loop(0, 2 * NA_KH - 2, pair, 0)


def _nat_bias(rpb):
    n_heads = rpb.shape[0]
    assert rpb.shape[1:] == (2 * NA_KH - 1, 2 * NA_KW - 1)
    shape = (n_heads, 2 * NA_KH - 2, GRID_W, 2 * GRID_W)
    return pl.pallas_call(
        _nat_bias_kernel,
        grid=(n_heads,),
        in_specs=[pl.BlockSpec(memory_space=pltpu.SMEM)],
        out_specs=pl.BlockSpec((1,) + shape[1:], lambda h: (h, 0, 0, 0)),
        out_shape=jax.ShapeDtypeStruct(shape, F32),
        compiler_params=_params(("parallel",), _nbytes(shape[1:], F32)),
        name="nat_bias_tiles",
    )(rpb.reshape(-1))


def _nat_kernel(q_ref, k_ref, v_ref, kc_ref, vc_ref, gq_ref, gk_ref, bias_ref, o_ref, kn_ref, *, rows, norm_rows):
    n_pairs = NA_KH // 2
    gq = gq_ref[...] * ATTN_SCALE
    gk = gk_ref[...]

    def norm_keys(i, carry):
        chunk = pl.ds(pl.multiple_of(i * norm_rows, norm_rows), norm_rows)
        kn_ref[chunk, :] = _rms_head(k_ref[0, chunk, :], gk).astype(BF16)
        return carry

    lax.fori_loop(0, rows * GRID_W // norm_rows, norm_keys, 0)
    kc = _rms_head(kc_ref[0], gk).astype(BF16)
    vc = vc_ref[0]
    n_ctx = kc.shape[0] // HEAD_DIM

    def scores(r):
        r0 = jnp.clip(r - NA_KH // 2, 0, rows - NA_KH)
        dr0 = r0 - r + (NA_KH - 1)
        q_rows = pl.ds(pl.multiple_of(r * GRID_W, GRID_W), GRID_W)
        k_rows = pl.ds(pl.multiple_of(r0 * GRID_W, GRID_W), NA_KH * GRID_W)
        q = _rms_head(q_ref[0, q_rows, :], gq).astype(BF16)
        return q_rows, k_rows, dr0, _nt_dot(q, kn_ref[k_rows, :]), _nt_dot(q, kc)

    def attend(q_rows, k_rows, dr0, s_nb, s_c):
        s = [s_nb[:, 2 * GRID_W * j:2 * GRID_W * (j + 1)] + bias_ref[0, dr0 + 2 * j] for j in range(n_pairs)]
        s += [s_c[:, HEAD_DIM * j:HEAD_DIM * (j + 1)] for j in range(n_ctx)]
        m = jnp.max(functools.reduce(jnp.maximum, s), -1, keepdims=True)
        p = [jnp.exp(sj - m) for sj in s]
        den = jnp.sum(functools.reduce(jnp.add, p), -1, keepdims=True)
        p_nb = jnp.concatenate(p[:n_pairs], axis=-1).astype(BF16)
        p_c = jnp.concatenate(p[n_pairs:], axis=-1).astype(BF16)
        o = (jnp.dot(p_nb, v_ref[0, k_rows, :], preferred_element_type=F32)
             + jnp.dot(p_c, vc, preferred_element_type=F32)) / den
        o_ref[0, q_rows, :] = o.astype(o_ref.dtype)

    def row_group(i, carry):
        staged = [scores(i * NAT_ROW_UNROLL + u) for u in range(NAT_ROW_UNROLL)]
        for row_state in staged:
            attend(*row_state)
        return carry

    lax.fori_loop(0, rows // NAT_ROW_UNROLL, row_group, 0)


def _nat(proj, projc, gq, gk, bias, *, n_heads, batch, seq):
    rows = seq // GRID_W
    assert seq % GRID_W == 0 and rows >= NA_KH and rows % NAT_ROW_UNROLL == 0
    lc = projc.shape[0] // batch
    assert lc % HEAD_DIM == 0
    norm_rows = _tile(seq, 512, 16)
    proj3 = proj.reshape(batch, seq, proj.shape[1])
    projc3 = projc.reshape(batch, lc, projc.shape[1])

    def head_block(n, first):
        return pl.BlockSpec((1, n, HEAD_DIM), lambda b, h: (b, 0, first + h))

    gain_block = pl.BlockSpec((1, HEAD_DIM), lambda b, h: (0, 0))
    blocks = (4 * _nbytes((seq, HEAD_DIM), BF16) + 2 * _nbytes((lc, HEAD_DIM), BF16) + _nbytes(bias.shape[1:], F32))
    temps = (_nbytes((seq, HEAD_DIM), BF16)
             + 4 * NAT_ROW_UNROLL * _nbytes((GRID_W, NA_KH * GRID_W + lc), F32) + 4 * _nbytes((norm_rows, HEAD_DIM), F32))
    out = pl.pallas_call(
        functools.partial(_nat_kernel, rows=rows, norm_rows=norm_rows),
        grid=(batch, n_heads),
        in_specs=[head_block(seq, 0), head_block(seq, n_heads), head_block(seq, 2 * n_heads),
                  head_block(lc, n_heads), head_block(lc, 2 * n_heads),
                  gain_block, gain_block,
                  pl.BlockSpec((1,) + bias.shape[1:], lambda b, h: (h, 0, 0, 0))],
        out_specs=head_block(seq, 0),
        out_shape=jax.ShapeDtypeStruct((batch, seq, n_heads * HEAD_DIM), BF16),
        scratch_shapes=[pltpu.VMEM((seq, HEAD_DIM), BF16)],
        compiler_params=_params(("parallel", "parallel"), blocks, temps),
        name="neighbourhood_attention",
    )(proj3, proj3, proj3, projc3, projc3, gq.reshape(1, HEAD_DIM), gk.reshape(1, HEAD_DIM), bias)
    return out.reshape(batch * seq, n_heads * HEAD_DIM)


def _gains(*groups):
    return jnp.concatenate([jnp.tile(g.astype(F32), n) for g, n in groups]).reshape(1, -1)


def kernel(x, c, ctx, c_ctx, w_ada, b_ada, norm_g, ffn_w_gate, ffn_w_up, ffn_w_down, mix_ab_w_in, mix_ab_w_out,
           gm_vnorm_g, gm_w_s, gm_b_s, swa_qnorm_g, swa_knorm_g, swa_sink, nat_w_qkv, nat_w_out, nat_qnorm_g,
           nat_knorm_g, nat_rpb):
    batch, seq, d = x.shape
    lc = ctx.shape[1]
    depth = w_ada.shape[0]
    ctx_row = batch
    assert batch + 1 <= MOD_ROWS

    cond = jnp.zeros((MOD_ROWS, d), F32).at[:batch].set(c).at[ctx_row].set(c_ctx)
    mod = _ada(cond, w_ada, b_ada).reshape(depth, MOD_ROWS, N_MOD, 1, d)

    lat_row = _ModRows(0, seq)
    ctx_row_of = _ModRows(ctx_row, batch * lc)

    def mod_vec(layer, sub, which):
        return mod[layer, :, 3 * sub + which]

    ffn_wg, ffn_wu, ffn_wd = (w.astype(BF16) for w in (ffn_w_gate, ffn_w_up, ffn_w_down))
    ab_w_in, ab_w_out = mix_ab_w_in.astype(BF16), mix_ab_w_out.astype(BF16)
    nat_qkv, nat_out = nat_w_qkv.astype(BF16), nat_w_out.astype(BF16)

    def ffn(xs, row, layer, sub, idx):
        h = _norm_mod(xs, norm_g[layer, sub], mod_vec(layer, sub, 0), mod_vec(layer, sub, 1), row)
        a = _gate_up(h, ffn_wg, ffn_wu, (layer, idx))
        return _residual_matmul([a], ffn_wd, (layer, idx), xs, mod_vec(layer, sub, 2), row, HALF_STEP)

    xs = x.reshape(batch * seq, d)
    cs = ctx.reshape(batch * lc, d)
    rope = _rope_tables(seq)

    for layer in range(depth):
        last = layer == depth - 1
        j = layer // 2
        xs = ffn(xs, lat_row, layer, 0, 0)
        cs = ffn(cs, ctx_row_of, layer, 0, 0)
        h = _norm_mod(xs, norm_g[layer, 1], mod_vec(layer, 1, 0), mod_vec(layer, 1, 1), lat_row)
        hc = _norm_mod(cs, norm_g[layer, 1], mod_vec(layer, 1, 0), mod_vec(layer, 1, 1), ctx_row_of)
        gate = mod_vec(layer, 1, 2)
        if layer % 2 == 0:
            n_groups = gm_w_s.shape[1]
            n_heads = swa_sink.shape[1]
            assert n_groups == n_heads
            q_col = 2 * n_groups
            v_col = q_col + n_heads + SWA_KV_HEADS
            proj = _matmul(h, ab_w_in, (j,))
            projc = _matmul(hc, ab_w_in, (j,))
            qk_gain = _gains((swa_qnorm_g[j] * ATTN_SCALE, n_heads), (swa_knorm_g[j], SWA_KV_HEADS))
            qk = _head_norm(proj, q_col, qk_gain, rope)
            qkc = _head_norm(projc, q_col, qk_gain)
            kc = qkc[:, n_heads * HEAD_DIM:]
            y_g = _gmlp(proj, gm_w_s[j], gm_b_s[j], gm_vnorm_g[j])
            y_a = _swa(qk, proj, kc, projc, swa_sink[j], n_heads=n_heads, v_col=v_col, batch=batch, seq=seq)
            xs = _residual_matmul([y_g, y_a], ab_w_out, (j,), xs, gate, lat_row, 1.0)
            if not last:
                yc_g = _gmlp(projc, gm_w_s[j], gm_b_s[j], gm_vnorm_g[j])
                yc_a = _ctx_attn(qkc, kc, projc, swa_sink[j], n_heads=n_heads, n_kv=SWA_KV_HEADS, v_col=v_col,
                                 batch=batch)
                cs = _residual_matmul([yc_g, yc_a], ab_w_out, (j,), cs, gate, ctx_row_of, 1.0)
        else:
            n_heads = nat_rpb.shape[1]
            proj = _matmul(h, nat_qkv, (j,))
            projc = _matmul(hc, nat_qkv, (j,))
            y = _nat(proj, projc, nat_qnorm_g[j], nat_knorm_g[j], _nat_bias(nat_rpb[j]), n_heads=n_heads,
                     batch=batch, seq=seq)
            xs = _residual_matmul([y], nat_out, (j,), xs, gate, lat_row, 1.0)
            if not last:
                qk_gain = _gains((nat_qnorm_g[j] * ATTN_SCALE, n_heads), (nat_knorm_g[j], n_heads))
                qkc = _head_norm(projc, 0, qk_gain)
                yc = _ctx_attn(qkc, qkc[:, n_heads * HEAD_DIM:], projc, None, n_heads=n_heads, n_kv=n_heads,
                               v_col=2 * n_heads, batch=batch)
                cs = _residual_matmul([yc], nat_out, (j,), cs, gate, ctx_row_of, 1.0)
        xs = ffn(xs, lat_row, layer, 2, 1)
        if not last:
            cs = ffn(cs, ctx_row_of, layer, 2, 1)
    return xs.reshape(batch, seq, d)
```

```python
import functools
import math
from typing import NamedTuple

import jax
import jax.numpy as jnp
from jax import lax
from jax.experimental import pallas as pl
from jax.experimental.pallas import tpu as pltpu

F32 = jnp.float32
BF16 = jnp.bfloat16

HEAD_DIM = 128
GRID_W = 64
CHUNK = 128
WINDOW = 128
BLOCK = 128
SWA_KV_HEADS = 4
NA_KH = 8
NA_KW = 16
NAT_ROW_UNROLL = 8
SWA_BLOCK_UNROLL = 2
HEAD_NORM_GROUP = 4
N_MOD = 9
HALF_STEP = 0.5
ROPE_THETA = 10000.0
NORM_EPS = 1e-6
NEG_INF = -1e30
ATTN_SCALE = HEAD_DIM ** -0.5

V7X_VMEM_BYTES = 64 * 1024 * 1024
VMEM_CEILING = V7X_VMEM_BYTES - 8 * 1024 * 1024
MOD_ROWS = 8
BF16_SUBLANES = 16
LHS_BLOCK_BUDGET = 8 * 1024 * 1024


def _vmem_limit(pipelined_bytes, temp_bytes=0):
    need = int((2 * pipelined_bytes + temp_bytes) * 1.25) + (2 << 20)
    return min(max(need, 16 << 20), VMEM_CEILING)


def _params(semantics, pipelined_bytes, temp_bytes=0):
    return pltpu.CompilerParams(dimension_semantics=semantics,
                                vmem_limit_bytes=_vmem_limit(pipelined_bytes, temp_bytes))


def _tile(n, pref, quantum):
    if n <= pref:
        return n
    t = (pref // quantum) * quantum
    while t > quantum and n % t:
        t -= quantum
    assert n % t == 0, (n, pref, quantum)
    return t


def _nbytes(shape, dtype):
    return math.prod(shape) * jnp.dtype(dtype).itemsize


def _ada_kernel(c_ref, w_ref, b_ref, o_ref):
    c = c_ref[...]
    s = (c * jax.nn.sigmoid(c)).astype(BF16)
    o_ref[0] = jnp.dot(s, w_ref[0].astype(BF16), preferred_element_type=F32) + b_ref[0]


def _ada(cond, w_ada, b_ada):
    depth, d, n = w_ada.shape
    tn = _tile(n, 512, 128)
    blocks = _nbytes((MOD_ROWS, d), F32) + _nbytes((d, tn), F32) + 2 * _nbytes((MOD_ROWS, tn), F32)
    return pl.pallas_call(
        _ada_kernel,
        grid=(depth, n // tn),
        in_specs=[pl.BlockSpec((MOD_ROWS, d), lambda l, j: (0, 0)),
                  pl.BlockSpec((1, d, tn), lambda l, j: (l, 0, j)),
                  pl.BlockSpec((1, 1, tn), lambda l, j: (l, 0, j))],
        out_specs=pl.BlockSpec((1, MOD_ROWS, tn), lambda l, j: (l, 0, j)),
        out_shape=jax.ShapeDtypeStruct((depth, MOD_ROWS, n), F32),
        compiler_params=_params(("parallel", "parallel"), blocks, _nbytes((d, tn), BF16)),
        name="ada_modulation",
    )(cond, w_ada, b_ada.reshape(depth, 1, n))


def _norm_mod_kernel(x_ref, g_ref, shift_ref, scale_ref, o_ref):
    x = x_ref[...]
    y = x * lax.rsqrt(jnp.mean(x * x, axis=-1, keepdims=True) + NORM_EPS)
    o_ref[...] = ((y * g_ref[...]) * (1.0 + scale_ref[0]) + shift_ref[0]).astype(o_ref.dtype)


class _ModRows:
    def __init__(self, first, span):
        self.first, self.span = first, span

    def of_tile(self, i, tm):
        assert self.span % tm == 0
        return self.first + (i * tm) // self.span


def _norm_mod(x, g, shift, scale, mod_rows):
    t, d = x.shape
    tm = _tile(mod_rows.span, 256, 8)
    mod_spec = pl.BlockSpec((1, 1, d), lambda i: (mod_rows.of_tile(i, tm), 0, 0))
    blocks = _nbytes((tm, d), F32) + _nbytes((tm, d), BF16) + 3 * _nbytes((1, d), F32)
    return pl.pallas_call(
        _norm_mod_kernel,
        grid=(t // tm,),
        in_specs=[pl.BlockSpec((tm, d), lambda i: (i, 0)),
                  pl.BlockSpec((1, d), lambda i: (0, 0)),
                  mod_spec, mod_spec],
        out_specs=pl.BlockSpec((tm, d), lambda i: (i, 0)),
        out_shape=jax.ShapeDtypeStruct((t, d), BF16),
        compiler_params=_params(("parallel",), blocks, 2 * _nbytes((tm, d), F32)),
        name="norm_modulate",
    )(x, g.reshape(1, d), shift, scale)


class _Weight(NamedTuple):
    array: jax.Array
    lead: tuple = ()

    @property
    def shape(self):
        return self.array.shape[-2:]


def _weight_spec(w, k, tn, k_block=0):
    return pl.BlockSpec((None,) * len(w.lead) + (k, tn), lambda i, j: w.lead + (k_block, j))


def _cast_rows(rows, n_steps):
    for r in range(BF16_SUBLANES, rows + 1, BF16_SUBLANES):
        if rows % r == 0 and rows // r <= n_steps:
            return r
    return None


class _SideCasts:
    def __init__(self, weights, grid):
        n_steps, n_inner = grid[0] * grid[1], grid[1]
        self.weights = list(weights)
        self.rows = [_cast_rows(w.shape[0], n_steps) for w in self.weights]
        self.carried = [w for w, r in zip(self.weights, self.rows) if r is not None]
        self.in_specs, self.out_specs, self.out_shapes, self.block_bytes = [], [], [], 0
        for w, r in zip(self.weights, self.rows):
            if r is None:
                continue
            rows, cols = w.shape
            block = functools.partial(lambda i, j, last: jnp.minimum(i * n_inner + j, last), last=rows // r - 1)
            self.in_specs.append(pl.BlockSpec((None,) * len(w.lead) + (r, cols),
                                              functools.partial(lambda i, j, w, block: w.lead + (block(i, j), 0),
                                                                w=w, block=block)))
            self.out_specs.append(pl.BlockSpec((r, cols), functools.partial(lambda i, j, block: (block(i, j), 0),
                                                                            block=block)))
            self.out_shapes.append(jax.ShapeDtypeStruct((rows, cols), BF16))
            self.block_bytes += _nbytes((r, cols), F32) + _nbytes((r, cols), BF16)

    @property
    def n(self):
        return len(self.carried)

    def inputs(self):
        return [w.array for w in self.carried]

    def results(self, outs):
        outs = iter(outs)
        return [_Weight(next(outs)) if r is not None else _Weight(w.array[w.lead].astype(BF16))
                for w, r in zip(self.weights, self.rows)]


def _run_side_casts(in_refs, out_refs):
    for src, dst in zip(in_refs, out_refs):
        dst[...] = src[...].astype(dst.dtype)


def _matmul_kernel(a_ref, w_ref, *rest, n_side):
    o_ref = rest[n_side]
    o_ref[...] = jnp.dot(a_ref[...], w_ref[...], preferred_element_type=F32).astype(o_ref.dtype)
    _run_side_casts(rest[:n_side], rest[n_side + 1:])


def _matmul(a, w, side_weights=()):
    t, k = a.shape
    n = w.shape[1]
    tm, tn = _tile(t, 1024, 8), _tile(n, 512, 128)
    grid = (t // tm, n // tn)
    side = _SideCasts(side_weights, grid)
    blocks = _nbytes((tm, k), BF16) + _nbytes((k, tn), BF16) + _nbytes((tm, tn), BF16) + side.block_bytes
    out, *casts = pl.pallas_call(
        functools.partial(_matmul_kernel, n_side=side.n),
        grid=grid,
        in_specs=[pl.BlockSpec((tm, k), lambda i, j: (i, 0)), _weight_spec(w, k, tn)] + side.in_specs,
        out_specs=[pl.BlockSpec((tm, tn), lambda i, j: (i, j))] + side.out_specs,
        out_shape=[jax.ShapeDtypeStruct((t, n), BF16)] + side.out_shapes,
        compiler_params=_params(("arbitrary", "arbitrary"), blocks, _nbytes((tm, tn), F32)),
        name="projection",
    )(a, w.array, *side.inputs())
    return out, side.results(casts)


def _gate_up_kernel(h_ref, wg_ref, wu_ref, *rest, n_side):
    o_ref = rest[n_side]
    h = h_ref[...]
    g = jnp.dot(h, wg_ref[...], preferred_element_type=F32)
    u = jnp.dot(h, wu_ref[...], preferred_element_type=F32)
    o_ref[...] = (g * jax.nn.sigmoid(g) * u).astype(o_ref.dtype)
    _run_side_casts(rest[:n_side], rest[n_side + 1:])


def _gate_up(h, wg, wu, side_weights=()):
    t, k = h.shape
    n = wg.shape[1]
    tm, tn = _tile(t, 1024, 8), _tile(n, 512, 128)
    grid = (t // tm, n // tn)
    side = _SideCasts(side_weights, grid)
    blocks = _nbytes((tm, k), BF16) + 2 * _nbytes((k, tn), BF16) + _nbytes((tm, tn), BF16) + side.block_bytes
    out, *casts = pl.pallas_call(
        functools.partial(_gate_up_kernel, n_side=side.n),
        grid=grid,
        in_specs=[pl.BlockSpec((tm, k), lambda i, j: (i, 0)), _weight_spec(wg, k, tn), _weight_spec(wu, k, tn)]
        + side.in_specs,
        out_specs=[pl.BlockSpec((tm, tn), lambda i, j: (i, j))] + side.out_specs,
        out_shape=[jax.ShapeDtypeStruct((t, n), BF16)] + side.out_shapes,
        compiler_params=_params(("arbitrary", "arbitrary"), blocks, 3 * _nbytes((tm, tn), F32)),
        name="ffn_gate_up",
    )(h, wg.array, wu.array, *side.inputs())
    return out, side.results(casts)


def _residual_kernel(*refs, n_parts, coef):
    a_refs, w_refs = refs[:n_parts], refs[n_parts:2 * n_parts]
    x_ref, gate_ref, o_ref = refs[2 * n_parts:]
    acc = jnp.dot(a_refs[0][...], w_refs[0][...], preferred_element_type=F32)
    for a_ref, w_ref in zip(a_refs[1:], w_refs[1:]):
        acc += jnp.dot(a_ref[...], w_ref[...], preferred_element_type=F32)
    o_ref[...] = x_ref[...] + (coef * gate_ref[0]) * acc


def _residual_matmul(parts, w, x, gate, mod_rows, coef):
    t, n = x.shape
    widths = [p.shape[1] for p in parts]
    assert len(set(widths)) == 1 and sum(widths) == w.shape[0]
    kp = widths[0]
    tm = _tile(mod_rows.span, 1024, 8)
    tn = _tile(n, 512 if _nbytes((tm, sum(widths)), BF16) <= LHS_BLOCK_BUDGET else 256, 128)
    blocks = (len(parts) * (_nbytes((tm, kp), BF16) + _nbytes((kp, tn), BF16))
              + 2 * _nbytes((tm, tn), F32) + _nbytes((1, tn), F32))
    in_specs = [pl.BlockSpec((tm, kp), lambda i, j: (i, 0)) for _ in parts]
    in_specs += [_weight_spec(w, kp, tn, k_block=p) for p in range(len(parts))]
    in_specs += [pl.BlockSpec((tm, tn), lambda i, j: (i, j)),
                 pl.BlockSpec((1, 1, tn), lambda i, j: (mod_rows.of_tile(i, tm), 0, j))]
    return pl.pallas_call(
        functools.partial(_residual_kernel, n_parts=len(parts), coef=coef),
        grid=(t // tm, n // tn),
        in_specs=in_specs,
        out_specs=pl.BlockSpec((tm, tn), lambda i, j: (i, j)),
        out_shape=jax.ShapeDtypeStruct((t, n), F32),
        compiler_params=_params(("parallel", "parallel"), blocks, _nbytes((tm, tn), F32)),
        name="residual_matmul",
    )(*parts, *([w.array] * len(parts)), x, gate)


def _gelu_tanh(x):
    return x * (0.5 * (1.0 + jnp.tanh(math.sqrt(2.0 / math.pi) * (x + 0.044715 * (x * x * x)))))


def _gmlp_kernel(uv_ref, ws_ref, bs_ref, vg_ref, o_ref, *, n_groups, n_chunks):
    for g in range(n_groups):
        u_cols = slice(g * HEAD_DIM, (g + 1) * HEAD_DIM)
        v_cols = slice((n_groups + g) * HEAD_DIM, (n_groups + g + 1) * HEAD_DIM)
        w = ws_ref[g]
        b = bs_ref[g]
        vg = vg_ref[:, u_cols]

        for c in range(n_chunks):
            rows = slice(c * CHUNK, (c + 1) * CHUNK)
            u = _gelu_tanh(uv_ref[rows, u_cols].astype(F32))
            v = _gelu_tanh(uv_ref[rows, v_cols].astype(F32))
            v = v - jnp.mean(v, axis=-1, keepdims=True)
            vn = (v * lax.rsqrt(jnp.mean(v * v, axis=-1, keepdims=True) + NORM_EPS)) * vg
            s = jnp.dot(w, vn.astype(BF16), preferred_element_type=F32) + b
            o_ref[rows, u_cols] = (u * s).astype(o_ref.dtype)


def _gmlp(proj, w_s, b_s, vnorm_g):
    t = proj.shape[0]
    n_groups = w_s.shape[0]
    gm = n_groups * HEAD_DIM
    tc = _tile(t, 512, CHUNK)
    blocks = (_nbytes((tc, 2 * gm), BF16) + _nbytes((tc, gm), BF16) + _nbytes(w_s.shape, BF16)
              + _nbytes((n_groups, CHUNK, 128), F32) + _nbytes((8, gm), F32))
    return pl.pallas_call(
        functools.partial(_gmlp_kernel, n_groups=n_groups, n_chunks=tc // CHUNK),
        grid=(t // tc,),
        in_specs=[pl.BlockSpec((tc, 2 * gm), lambda i: (i, 0)),
                  pl.BlockSpec(w_s.shape, lambda i: (0, 0, 0)),
                  pl.BlockSpec((n_groups, CHUNK, 1), lambda i: (0, 0, 0)),
                  pl.BlockSpec((1, gm), lambda i: (0, 0))],
        out_specs=pl.BlockSpec((tc, gm), lambda i: (i, 0)),
        out_shape=jax.ShapeDtypeStruct((t, gm), BF16),
        compiler_params=_params(("parallel",), blocks, 16 * _nbytes((CHUNK, HEAD_DIM), F32)),
        name="gmlp_chunk_gating",
    )(proj, w_s.astype(BF16), b_s.reshape(n_groups, CHUNK, 1), vnorm_g.reshape(1, gm))


def _swap_rope_halves(x):
    lane = lax.broadcasted_iota(jnp.int32, x.shape, x.ndim - 1)
    return jnp.where(lane % 64 < 32, pltpu.roll(x, HEAD_DIM - 32, x.ndim - 1), pltpu.roll(x, 32, x.ndim - 1))


def _rms_head(x, g):
    x = x.astype(F32)
    return (x * lax.rsqrt(jnp.mean(x * x, axis=-1, keepdims=True) + NORM_EPS)) * g


def _head_norm_kernel(x_ref, g_ref, *rest, rope, heads):
    o_ref = rest[-1]
    for h in range(heads):
        cols = slice(h * HEAD_DIM, (h + 1) * HEAD_DIM)
        y = _rms_head(x_ref[:, cols], g_ref[:, cols])
        if rope:
            cos_ref, sin_ref = rest[:2]
            y = y * cos_ref[...] + _swap_rope_halves(y) * sin_ref[...]
        o_ref[:, cols] = y.astype(o_ref.dtype)


def _head_norm(src, first_head, gains, rope_tables=None):
    t = src.shape[0]
    n_heads = gains.shape[1] // HEAD_DIM
    heads = HEAD_NORM_GROUP if first_head % HEAD_NORM_GROUP == 0 and n_heads % HEAD_NORM_GROUP == 0 else 1
    width = heads * HEAD_DIM
    seq = rope_tables[0].shape[0] if rope_tables is not None else t
    tm = _tile(seq, 1024, 16)
    per_seq = seq // tm
    in_specs = [pl.BlockSpec((tm, width), lambda i, h: (i, first_head // heads + h)),
                pl.BlockSpec((1, width), lambda i, h: (0, h))]
    args = [src, gains]
    if rope_tables is not None:
        in_specs += [pl.BlockSpec((tm, HEAD_DIM), lambda i, h: (i % per_seq, 0))] * 2
        args += list(rope_tables)
    blocks = 2 * _nbytes((tm, width), BF16) + 2 * _nbytes((tm, HEAD_DIM), F32)
    return pl.pallas_call(
        functools.partial(_head_norm_kernel, rope=rope_tables is not None, heads=heads),
        grid=(t // tm, n_heads // heads),
        in_specs=in_specs,
        out_specs=pl.BlockSpec((tm, width), lambda i, h: (i, h)),
        out_shape=jax.ShapeDtypeStruct((t, n_heads * HEAD_DIM), BF16),
        compiler_params=_params(("parallel", "parallel"), blocks, 6 * heads * _nbytes((tm, HEAD_DIM), F32)),
        name="head_norm_rope" if rope_tables is not None else "head_norm",
    )(*args)


def _rope_tables(seq):
    half = HEAD_DIM // 2
    t = jnp.arange(seq)
    rows = (t // GRID_W).astype(F32)
    cols = (t % GRID_W).astype(F32)
    inv = 1.0 / (ROPE_THETA ** (jnp.arange(0, half, 2, dtype=F32) / half))
    ang = jnp.concatenate([rows[:, None] * inv, rows[:, None] * inv, cols[:, None] * inv, cols[:, None] * inv], -1)
    sign = jnp.where(jnp.arange(HEAD_DIM) % half < half // 2, -1.0, 1.0).astype(F32)
    return jnp.cos(ang), jnp.sin(ang) * sign


def _nt_dot(a, b):
    return lax.dot_general(a, b, (((1,), (1,)), ((), ())), preferred_element_type=F32)


def _sink_column(sink_ref, first, group, rows_per_head):
    head = lax.broadcasted_iota(jnp.int32, (group * rows_per_head, 1), 0) // rows_per_head
    col = jnp.zeros((group * rows_per_head, 1), F32)
    for g in range(group):
        col = jnp.where(head == g, sink_ref[first + g], col)
    return col


def _swa_kernel(sink_ref, q_ref, k_ref, v_ref, kc_ref, vc_ref, o_ref, *, tq, seq, group):
    kv_head, q_tile = pl.program_id(1), pl.program_id(2)
    kc, vc = kc_ref[0], vc_ref[0]
    sink = _sink_column(sink_ref, kv_head * group, group, BLOCK)
    q_off = lax.broadcasted_iota(jnp.int32, (group * BLOCK, 1), 0) % BLOCK
    k_off = lax.broadcasted_iota(jnp.int32, (1, 3 * BLOCK), 1)

    def scores(j):
        base = q_tile * tq + j * BLOCK
        start = pl.multiple_of(jnp.clip(base - BLOCK, 0, seq - 3 * BLOCK), BLOCK)
        q_rows = pl.ds(pl.multiple_of(j * BLOCK, BLOCK), BLOCK)
        q = jnp.concatenate([q_ref[0, q_rows, g * HEAD_DIM:(g + 1) * HEAD_DIM] for g in range(group)], axis=0)
        s_w = _nt_dot(q, k_ref[0, pl.ds(start, 3 * BLOCK), :])
        return base, start, q_rows, s_w, _nt_dot(q, kc)

    def attend(base, start, q_rows, s_w, s_c):
        s_w = jnp.where(jnp.abs((start + k_off) - (base + q_off)) <= WINDOW, s_w, NEG_INF)
        m = jnp.maximum(jnp.maximum(jnp.max(s_w, -1, keepdims=True), jnp.max(s_c, -1, keepdims=True)), sink)
        p_w = jnp.exp(s_w - m)
        p_c = jnp.exp(s_c - m)
        den = jnp.sum(p_w, -1, keepdims=True) + jnp.sum(p_c, -1, keepdims=True) + jnp.exp(sink - m)
        o = (jnp.dot(p_w.astype(BF16), v_ref[0, pl.ds(start, 3 * BLOCK), :], preferred_element_type=F32)
             + jnp.dot(p_c.astype(BF16), vc, preferred_element_type=F32)) / den
        for g in range(group):
            o_ref[0, q_rows, g * HEAD_DIM:(g + 1) * HEAD_DIM] = o[g * BLOCK:(g + 1) * BLOCK].astype(o_ref.dtype)

    def block_group(i, carry):
        staged = [scores(i * SWA_BLOCK_UNROLL + u) for u in range(SWA_BLOCK_UNROLL)]
        for block_state in staged:
            attend(*block_state)
        return carry

    assert (tq // BLOCK) % SWA_BLOCK_UNROLL == 0
    lax.fori_loop(0, tq // BLOCK // SWA_BLOCK_UNROLL, block_group, 0)


def _swa(qk, proj, kc, projc, sink, *, n_heads, v_col, batch, seq):
    group = n_heads // SWA_KV_HEADS
    lc = kc.shape[0] // batch
    gw = group * HEAD_DIM
    tq = _tile(seq, 1024, BLOCK)
    qk3 = qk.reshape(batch, seq, qk.shape[1])
    proj3 = proj.reshape(batch, seq, proj.shape[1])
    kc3 = kc.reshape(batch, lc, kc.shape[1])
    projc3 = projc.reshape(batch, lc, projc.shape[1])
    blocks = (2 * _nbytes((tq, gw), BF16) + 2 * _nbytes((seq, HEAD_DIM), BF16) + 2 * _nbytes((lc, HEAD_DIM), BF16))
    temps = 6 * _nbytes((group * BLOCK, 3 * BLOCK + lc), F32)
    out = pl.pallas_call(
        functools.partial(_swa_kernel, tq=tq, seq=seq, group=group),
        grid=(batch, SWA_KV_HEADS, seq // tq),
        in_specs=[pl.BlockSpec(memory_space=pltpu.SMEM),
                  pl.BlockSpec((1, tq, gw), lambda b, h, i: (b, i, h)),
                  pl.BlockSpec((1, seq, HEAD_DIM), lambda b, h, i: (b, 0, n_heads + h)),
                  pl.BlockSpec((1, seq, HEAD_DIM), lambda b, h, i: (b, 0, v_col + h)),
                  pl.BlockSpec((1, lc, HEAD_DIM), lambda b, h, i: (b, 0, h)),
                  pl.BlockSpec((1, lc, HEAD_DIM), lambda b, h, i: (b, 0, v_col + h))],
        out_specs=pl.BlockSpec((1, tq, gw), lambda b, h, i: (b, i, h)),
        out_shape=jax.ShapeDtypeStruct((batch, seq, n_heads * HEAD_DIM), BF16),
        compiler_params=_params(("parallel", "parallel", "parallel"), blocks, temps),
        name="window_attention",
    )(sink, qk3, qk3, proj3, kc3, projc3)
    return out.reshape(batch * seq, n_heads * HEAD_DIM)


def _ctx_attn_kernel(*refs, group, lc, use_sink):
    if use_sink:
        sink_ref, q_ref, k_ref, v_ref, o_ref = refs
    else:
        q_ref, k_ref, v_ref, o_ref = refs
    q = jnp.concatenate([q_ref[0, :, g * HEAD_DIM:(g + 1) * HEAD_DIM] for g in range(group)], axis=0)
    s = _nt_dot(q, k_ref[0])
    m = jnp.max(s, -1, keepdims=True)
    if use_sink:
        sink = _sink_column(sink_ref, pl.program_id(1) * group, group, lc)
        m = jnp.maximum(m, sink)
    p = jnp.exp(s - m)
    den = jnp.sum(p, -1, keepdims=True)
    if use_sink:
        den = den + jnp.exp(sink - m)
    o = jnp.dot(p.astype(BF16), v_ref[0], preferred_element_type=F32) / den
    for g in range(group):
        o_ref[0, :, g * HEAD_DIM:(g + 1) * HEAD_DIM] = o[g * lc:(g + 1) * lc].astype(o_ref.dtype)


def _ctx_attn(qc, kc, projc, sink, *, n_heads, n_kv, v_col, batch):
    group = n_heads // n_kv
    lc = qc.shape[0] // batch
    gw = group * HEAD_DIM
    qc3 = qc.reshape(batch, lc, qc.shape[1])
    kc3 = kc.reshape(batch, lc, kc.shape[1])
    projc3 = projc.reshape(batch, lc, projc.shape[1])
    use_sink = sink is not None
    in_specs = [pl.BlockSpec((1, lc, gw), lambda b, h: (b, 0, h)),
                pl.BlockSpec((1, lc, HEAD_DIM), lambda b, h: (b, 0, h)),
                pl.BlockSpec((1, lc, HEAD_DIM), lambda b, h: (b, 0, v_col + h))]
    args = [qc3, kc3, projc3]
    if use_sink:
        in_specs = [pl.BlockSpec(memory_space=pltpu.SMEM)] + in_specs
        args = [sink] + args
    blocks = 2 * _nbytes((lc, gw), BF16) + 2 * _nbytes((lc, HEAD_DIM), BF16)
    out = pl.pallas_call(
        functools.partial(_ctx_attn_kernel, group=group, lc=lc, use_sink=use_sink),
        grid=(batch, n_kv),
        in_specs=in_specs,
        out_specs=pl.BlockSpec((1, lc, gw), lambda b, h: (b, 0, h)),
        out_shape=jax.ShapeDtypeStruct((batch, lc, n_heads * HEAD_DIM), BF16),
        compiler_params=_params(("parallel", "parallel"), blocks, 6 * _nbytes((group * lc, lc), F32)),
        name="context_attention",
    )(*args)
    return out.reshape(batch * lc, n_heads * HEAD_DIM)


def _nat_bias_kernel(rpb_ref, o_ref):
    h = pl.program_id(0)
    n_dc = 2 * NA_KW - 1
    w = lax.broadcasted_iota(jnp.int32, (GRID_W, 2 * GRID_W), 0)
    lane = lax.broadcasted_iota(jnp.int32, (GRID_W, 2 * GRID_W), 1)
    second = lane >= GRID_W
    kc = lane % GRID_W
    c0 = jnp.clip(w - NA_KW // 2, 0, GRID_W - NA_KW)
    dc = jnp.where((kc >= c0) & (kc < c0 + NA_KW), kc - w + (NA_KW - 1), -1)

    def pair(dr, carry):
        base = (h * (2 * NA_KH - 1) + dr) * n_dc
        tile = jnp.full((GRID_W, 2 * GRID_W), NEG_INF, F32)
        for j in range(n_dc):
            tile = jnp.where(dc == j, jnp.where(second, rpb_ref[base + n_dc + j], rpb_ref[base + j]), tile)
        o_ref[0, dr] = tile
        return carry

    lax.fori_loop(0, 2 * NA_KH - 2, pair, 0)


def _nat_bias(rpb):
    n_heads = rpb.shape[0]
    assert rpb.shape[1:] == (2 * NA_KH - 1, 2 * NA_KW - 1)
    shape = (n_heads, 2 * NA_KH - 2, GRID_W, 2 * GRID_W)
    return pl.pallas_call(
        _nat_bias_kernel,
        grid=(n_heads,),
        in_specs=[pl.BlockSpec(memory_space=pltpu.SMEM)],
        out_specs=pl.BlockSpec((1,) + shape[1:], lambda h: (h, 0, 0, 0)),
        out_shape=jax.ShapeDtypeStruct(shape, F32),
        compiler_params=_params(("parallel",), _nbytes(shape[1:], F32)),
        name="nat_bias_tiles",
    )(rpb.reshape(-1))


def _nat_kernel(q_ref, k_ref, v_ref, kc_ref, vc_ref, gq_ref, gk_ref, bias_ref, o_ref, kn_ref, *, rows, norm_rows):
    n_pairs = NA_KH // 2
    gq = gq_ref[...] * ATTN_SCALE
    gk = gk_ref[...]

    def norm_keys(i, carry):
        chunk = pl.ds(pl.multiple_of(i * norm_rows, norm_rows), norm_rows)
        kn_ref[chunk, :] = _rms_head(k_ref[0, chunk, :], gk).astype(BF16)
        return carry

    lax.fori_loop(0, rows * GRID_W // norm_rows, norm_keys, 0)
    kc = _rms_head(kc_ref[0], gk).astype(BF16)
    vc = vc_ref[0]
    n_ctx = kc.shape[0] // HEAD_DIM

    def scores(r):
        r0 = jnp.clip(r - NA_KH // 2, 0, rows - NA_KH)
        dr0 = r0 - r + (NA_KH - 1)
        q_rows = pl.ds(pl.multiple_of(r * GRID_W, GRID_W), GRID_W)
        k_rows = pl.ds(pl.multiple_of(r0 * GRID_W, GRID_W), NA_KH * GRID_W)
        q = _rms_head(q_ref[0, q_rows, :], gq).astype(BF16)
        return q_rows, k_rows, dr0, _nt_dot(q, kn_ref[k_rows, :]), _nt_dot(q, kc)

    def attend(q_rows, k_rows, dr0, s_nb, s_c):
        s = [s_nb[:, 2 * GRID_W * j:2 * GRID_W * (j + 1)] + bias_ref[0, dr0 + 2 * j] for j in range(n_pairs)]
        s += [s_c[:, HEAD_DIM * j:HEAD_DIM * (j + 1)] for j in range(n_ctx)]
        m = jnp.max(functools.reduce(jnp.maximum, s), -1, keepdims=True)
        p = [jnp.exp(sj - m) for sj in s]
        den = jnp.sum(functools.reduce(jnp.add, p), -1, keepdims=True)
        p_nb = jnp.concatenate(p[:n_pairs], axis=-1).astype(BF16)
        p_c = jnp.concatenate(p[n_pairs:], axis=-1).astype(BF16)
        o = (jnp.dot(p_nb, v_ref[0, k_rows, :], preferred_element_type=F32)
             + jnp.dot(p_c, vc, preferred_element_type=F32)) / den
        o_ref[0, q_rows, :] = o.astype(o_ref.dtype)

    def row_group(i, carry):
        staged = [scores(i * NAT_ROW_UNROLL + u) for u in range(NAT_ROW_UNROLL)]
        for row_state in staged:
            attend(*row_state)
        return carry

    lax.fori_loop(0, rows // NAT_ROW_UNROLL, row_group, 0)


def _nat(proj, projc, gq, gk, bias, *, n_heads, batch, seq):
    rows = seq // GRID_W
    assert seq % GRID_W == 0 and rows >= NA_KH and rows % NAT_ROW_UNROLL == 0
    lc = projc.shape[0] // batch
    assert lc % HEAD_DIM == 0
    norm_rows = _tile(seq, 512, 16)
    proj3 = proj.reshape(batch, seq, proj.shape[1])
    projc3 = projc.reshape(batch, lc, projc.shape[1])

    def head_block(n, first):
        return pl.BlockSpec((1, n, HEAD_DIM), lambda b, h: (b, 0, first + h))

    gain_block = pl.BlockSpec((1, HEAD_DIM), lambda b, h: (0, 0))
    blocks = (4 * _nbytes((seq, HEAD_DIM), BF16) + 2 * _nbytes((lc, HEAD_DIM), BF16) + _nbytes(bias.shape[1:], F32))
    temps = (_nbytes((seq, HEAD_DIM), BF16)
             + 4 * NAT_ROW_UNROLL * _nbytes((GRID_W, NA_KH * GRID_W + lc), F32) + 4 * _nbytes((norm_rows, HEAD_DIM), F32))
    out = pl.pallas_call(
        functools.partial(_nat_kernel, rows=rows, norm_rows=norm_rows),
        grid=(batch, n_heads),
        in_specs=[head_block(seq, 0), head_block(seq, n_heads), head_block(seq, 2 * n_heads),
                  head_block(lc, n_heads), head_block(lc, 2 * n_heads),
                  gain_block, gain_block,
                  pl.BlockSpec((1,) + bias.shape[1:], lambda b, h: (h, 0, 0, 0))],
        out_specs=head_block(seq, 0),
        out_shape=jax.ShapeDtypeStruct((batch, seq, n_heads * HEAD_DIM), BF16),
        scratch_shapes=[pltpu.VMEM((seq, HEAD_DIM), BF16)],
        compiler_params=_params(("parallel", "parallel"), blocks, temps),
        name="neighbourhood_attention",
    )(proj3, proj3, proj3, projc3, projc3, gq.reshape(1, HEAD_DIM), gk.reshape(1, HEAD_DIM), bias)
    return out.reshape(batch * seq, n_heads * HEAD_DIM)


def _gains(*groups):
    return jnp.concatenate([jnp.tile(g.astype(F32), n) for g, n in groups]).reshape(1, -1)


def kernel(x, c, ctx, c_ctx, w_ada, b_ada, norm_g, ffn_w_gate, ffn_w_up, ffn_w_down, mix_ab_w_in, mix_ab_w_out,
           gm_vnorm_g, gm_w_s, gm_b_s, swa_qnorm_g, swa_knorm_g, swa_sink, nat_w_qkv, nat_w_out, nat_qnorm_g,
           nat_knorm_g, nat_rpb):
    batch, seq, d = x.shape
    lc = ctx.shape[1]
    depth = w_ada.shape[0]
    ctx_row = batch
    assert batch + 1 <= MOD_ROWS

    cond = jnp.zeros((MOD_ROWS, d), F32).at[:batch].set(c).at[ctx_row].set(c_ctx)
    mod = _ada(cond, w_ada, b_ada).reshape(depth, MOD_ROWS, N_MOD, 1, d)

    lat_row = _ModRows(0, seq)
    ctx_row_of = _ModRows(ctx_row, batch * lc)

    def mod_vec(layer, sub, which):
        return mod[layer, :, 3 * sub + which]

    def ffn_in(xs, row, layer, sub):
        return _norm_mod(xs, norm_g[layer, sub], mod_vec(layer, sub, 0), mod_vec(layer, sub, 1), row)

    def ffn_out(a, wd, xs, row, layer, sub):
        return _residual_matmul([a], wd, xs, mod_vec(layer, sub, 2), row, HALF_STEP)

    def ffn_ctx(cs, layer, sub, wg, wu, wd):
        a, _ = _gate_up(ffn_in(cs, ctx_row_of, layer, sub), wg, wu)
        return ffn_out(a, wd, cs, ctx_row_of, layer, sub)

    def ffn_f32(layer, idx):
        return (_Weight(w, (layer, idx)) for w in (ffn_w_gate, ffn_w_up, ffn_w_down))

    xs = x.reshape(batch * seq, d)
    cs = ctx.reshape(batch * lc, d)
    rope = _rope_tables(seq)

    wg0_f32, wu0_f32, _ = ffn_f32(0, 0)
    wg, wu = (_Weight(w.array[w.lead].astype(BF16)) for w in (wg0_f32, wu0_f32))

    for layer in range(depth):
        last = layer == depth - 1
        j = layer // 2
        even = layer % 2 == 0
        mix_in_f32 = _Weight(mix_ab_w_in if even else nat_w_qkv, (j,))
        mix_out_f32 = _Weight(mix_ab_w_out if even else nat_w_out, (j,))
        _, _, wd_f32 = ffn_f32(layer, 0)
        a, (wd, mix_in, mix_out) = _gate_up(ffn_in(xs, lat_row, layer, 0), wg, wu, [wd_f32, mix_in_f32, mix_out_f32])
        xs = ffn_out(a, wd, xs, lat_row, layer, 0)
        cs = ffn_ctx(cs, layer, 0, wg, wu, wd)
        h = _norm_mod(xs, norm_g[layer, 1], mod_vec(layer, 1, 0), mod_vec(layer, 1, 1), lat_row)
        hc = _norm_mod(cs, norm_g[layer, 1], mod_vec(layer, 1, 0), mod_vec(layer, 1, 1), ctx_row_of)
        gate = mod_vec(layer, 1, 2)
        wg1_f32, wu1_f32, wd1_f32 = ffn_f32(layer, 1)
        proj, (wg, wu) = _matmul(h, mix_in, [wg1_f32, wu1_f32])
        projc, _ = _matmul(hc, mix_in)
        if even:
            n_groups = gm_w_s.shape[1]
            n_heads = swa_sink.shape[1]
            assert n_groups == n_heads
            q_col = 2 * n_groups
            v_col = q_col + n_heads + SWA_KV_HEADS
            qk_gain = _gains((swa_qnorm_g[j] * ATTN_SCALE, n_heads), (swa_knorm_g[j], SWA_KV_HEADS))
            qk = _head_norm(proj, q_col, qk_gain, rope)
            qkc = _head_norm(projc, q_col, qk_gain)
            kc = qkc[:, n_heads * HEAD_DIM:]
            y_g = _gmlp(proj, gm_w_s[j], gm_b_s[j], gm_vnorm_g[j])
            y_a = _swa(qk, proj, kc, projc, swa_sink[j], n_heads=n_heads, v_col=v_col, batch=batch, seq=seq)
            xs = _residual_matmul([y_g, y_a], mix_out, xs, gate, lat_row, 1.0)
            if not last:
                yc_g = _gmlp(projc, gm_w_s[j], gm_b_s[j], gm_vnorm_g[j])
                yc_a = _ctx_attn(qkc, kc, projc, swa_sink[j], n_heads=n_heads, n_kv=SWA_KV_HEADS, v_col=v_col,
                                 batch=batch)
                cs = _residual_matmul([yc_g, yc_a], mix_out, cs, gate, ctx_row_of, 1.0)
        else:
            n_heads = nat_rpb.shape[1]
            y = _nat(proj, projc, nat_qnorm_g[j], nat_knorm_g[j], _nat_bias(nat_rpb[j]), n_heads=n_heads,
                     batch=batch, seq=seq)
            xs = _residual_matmul([y], mix_out, xs, gate, lat_row, 1.0)
            if not last:
                qk_gain = _gains((nat_qnorm_g[j] * ATTN_SCALE, n_heads), (nat_knorm_g[j], n_heads))
                qkc = _head_norm(projc, 0, qk_gain)
                yc = _ctx_attn(qkc, qkc[:, n_heads * HEAD_DIM:], projc, None, n_heads=n_heads, n_kv=n_heads,
                               v_col=2 * n_heads, batch=batch)
                cs = _residual_matmul([yc], mix_out, cs, gate, ctx_row_of, 1.0)
        next_f32 = [] if last else list(ffn_f32(layer + 1, 0))[:2]
        a, (wd1, *next_w) = _gate_up(ffn_in(xs, lat_row, layer, 2), wg, wu, [wd1_f32] + next_f32)
        xs = ffn_out(a, wd1, xs, lat_row, layer, 2)
        if not last:
            cs = ffn_ctx(cs, layer, 2, wg, wu, wd1)
            wg, wu = next_w
    return xs.reshape(batch, seq, d)
```

```python
import functools
import math
from typing import NamedTuple

import jax
import jax.numpy as jnp
from jax import lax
from jax.experimental import pallas as pl
from jax.experimental.pallas import tpu as pltpu

F32 = jnp.float32
BF16 = jnp.bfloat16

HEAD_DIM = 128
GRID_W = 64
CHUNK = 128
WINDOW = 128
BLOCK = 128
SWA_KV_HEADS = 4
NA_KH = 8
NA_KW = 16
NAT_ROW_UNROLL = 8
NAT_HEADS_PER_STEP = 2
SWA_BLOCK_UNROLL = 2
HEAD_NORM_GROUP = 4
N_MOD = 9
HALF_STEP = 0.5
ROPE_THETA = 10000.0
NORM_EPS = 1e-6
NEG_INF = -1e30
ATTN_SCALE = HEAD_DIM ** -0.5

V7X_VMEM_BYTES = 64 * 1024 * 1024
VMEM_CEILING = V7X_VMEM_BYTES - 8 * 1024 * 1024
MOD_ROWS = 8
BF16_SUBLANES = 16
NORM_ROW_GROUP = BF16_SUBLANES
NORM_GROUP_UNROLL = 4
LHS_BLOCK_BUDGET = 8 * 1024 * 1024


def _vmem_limit(pipelined_bytes, temp_bytes=0):
    need = int((2 * pipelined_bytes + temp_bytes) * 1.25) + (2 << 20)
    return min(max(need, 16 << 20), VMEM_CEILING)


def _params(semantics, pipelined_bytes, temp_bytes=0):
    return pltpu.CompilerParams(dimension_semantics=semantics,
                                vmem_limit_bytes=_vmem_limit(pipelined_bytes, temp_bytes))


def _tile(n, pref, quantum):
    if n <= pref:
        return n
    t = (pref // quantum) * quantum
    while t > quantum and n % t:
        t -= quantum
    assert n % t == 0, (n, pref, quantum)
    return t


def _nbytes(shape, dtype):
    return math.prod(shape) * jnp.dtype(dtype).itemsize


def _ada_kernel(c_ref, w_ref, b_ref, o_ref):
    c = c_ref[...]
    s = (c * jax.nn.sigmoid(c)).astype(BF16)
    o_ref[0] = jnp.dot(s, w_ref[0].astype(BF16), preferred_element_type=F32) + b_ref[0]


def _ada(cond, w_ada, b_ada):
    depth, d, n = w_ada.shape
    tn = _tile(n, 1024, 128)
    blocks = _nbytes((MOD_ROWS, d), F32) + _nbytes((d, tn), F32) + 2 * _nbytes((MOD_ROWS, tn), F32)
    return pl.pallas_call(
        _ada_kernel,
        grid=(depth, n // tn),
        in_specs=[pl.BlockSpec((MOD_ROWS, d), lambda l, j: (0, 0)),
                  pl.BlockSpec((1, d, tn), lambda l, j: (l, 0, j)),
                  pl.BlockSpec((1, 1, tn), lambda l, j: (l, 0, j))],
        out_specs=pl.BlockSpec((1, MOD_ROWS, tn), lambda l, j: (l, 0, j)),
        out_shape=jax.ShapeDtypeStruct((depth, MOD_ROWS, n), F32),
        compiler_params=_params(("parallel", "parallel"), blocks, _nbytes((d, tn), BF16)),
        name="ada_modulation",
    )(cond, w_ada, b_ada.reshape(depth, 1, n))


def _norm_mod_kernel(x_ref, g_ref, shift_ref, scale_ref, o_ref):
    gain = g_ref[...] * (1.0 + scale_ref[0])
    shift = shift_ref[0]

    def row_group(i, carry):
        rows = pl.ds(pl.multiple_of(i * NORM_ROW_GROUP, NORM_ROW_GROUP), NORM_ROW_GROUP)
        x = x_ref[rows, :]
        y = x * lax.rsqrt(jnp.mean(x * x, axis=-1, keepdims=True) + NORM_EPS)
        o_ref[rows, :] = (y * gain + shift).astype(o_ref.dtype)
        return carry

    lax.fori_loop(0, x_ref.shape[0] // NORM_ROW_GROUP, row_group, 0, unroll=NORM_GROUP_UNROLL)


class _ModRows:
    def __init__(self, first, span):
        self.first, self.span = first, span

    def of_tile(self, i, tm):
        assert self.span % tm == 0
        return self.first + (i * tm) // self.span


def _norm_mod(x, g, shift, scale, mod_rows):
    t, d = x.shape
    tm = _tile(mod_rows.span, 256, NORM_ROW_GROUP)
    mod_spec = pl.BlockSpec((1, 1, d), lambda i: (mod_rows.of_tile(i, tm), 0, 0))
    blocks = _nbytes((tm, d), F32) + _nbytes((tm, d), BF16) + 3 * _nbytes((1, d), F32)
    return pl.pallas_call(
        _norm_mod_kernel,
        grid=(t // tm,),
        in_specs=[pl.BlockSpec((tm, d), lambda i: (i, 0)),
                  pl.BlockSpec((1, d), lambda i: (0, 0)),
                  mod_spec, mod_spec],
        out_specs=pl.BlockSpec((tm, d), lambda i: (i, 0)),
        out_shape=jax.ShapeDtypeStruct((t, d), BF16),
        compiler_params=_params(("parallel",), blocks, 2 * _nbytes((tm, d), F32)),
        name="norm_modulate",
    )(x, g.reshape(1, d), shift, scale)


class _Weight(NamedTuple):
    array: jax.Array
    lead: tuple = ()

    @property
    def shape(self):
        return self.array.shape[-2:]


def _weight_spec(w, k, tn, k_block=0):
    return pl.BlockSpec((None,) * len(w.lead) + (k, tn), lambda i, j: w.lead + (k_block, j))


def _cast_rows(rows, n_steps):
    for r in range(BF16_SUBLANES, rows + 1, BF16_SUBLANES):
        if rows % r == 0 and rows // r <= n_steps:
            return r
    return None


class _SideCasts:
    def __init__(self, weights, grid):
        n_steps, n_inner = grid[0] * grid[1], grid[1]
        self.weights = list(weights)
        self.rows = [_cast_rows(w.shape[0], n_steps) for w in self.weights]
        self.carried = [w for w, r in zip(self.weights, self.rows) if r is not None]
        self.in_specs, self.out_specs, self.out_shapes, self.block_bytes = [], [], [], 0
        for w, r in zip(self.weights, self.rows):
            if r is None:
                continue
            rows, cols = w.shape
            block = functools.partial(lambda i, j, last: jnp.minimum(i * n_inner + j, last), last=rows // r - 1)
            self.in_specs.append(pl.BlockSpec((None,) * len(w.lead) + (r, cols),
                                              functools.partial(lambda i, j, w, block: w.lead + (block(i, j), 0),
                                                                w=w, block=block)))
            self.out_specs.append(pl.BlockSpec((r, cols), functools.partial(lambda i, j, block: (block(i, j), 0),
                                                                            block=block)))
            self.out_shapes.append(jax.ShapeDtypeStruct((rows, cols), BF16))
            self.block_bytes += _nbytes((r, cols), F32) + _nbytes((r, cols), BF16)

    @property
    def n(self):
        return len(self.carried)

    def inputs(self):
        return [w.array for w in self.carried]

    def results(self, outs):
        outs = iter(outs)
        return [_Weight(next(outs)) if r is not None else _Weight(w.array[w.lead].astype(BF16))
                for w, r in zip(self.weights, self.rows)]


def _run_side_casts(in_refs, out_refs):
    for src, dst in zip(in_refs, out_refs):
        dst[...] = src[...].astype(dst.dtype)


def _matmul_kernel(a_ref, w_ref, *rest, n_side):
    o_ref = rest[n_side]
    o_ref[...] = jnp.dot(a_ref[...], w_ref[...], preferred_element_type=F32).astype(o_ref.dtype)
    _run_side_casts(rest[:n_side], rest[n_side + 1:])


def _matmul(a, w, side_weights=()):
    t, k = a.shape
    n = w.shape[1]
    tm, tn = _tile(t, 1024, 8), _tile(n, 512, 128)
    grid = (t // tm, n // tn)
    side = _SideCasts(side_weights, grid)
    blocks = _nbytes((tm, k), BF16) + _nbytes((k, tn), BF16) + _nbytes((tm, tn), BF16) + side.block_bytes
    out, *casts = pl.pallas_call(
        functools.partial(_matmul_kernel, n_side=side.n),
        grid=grid,
        in_specs=[pl.BlockSpec((tm, k), lambda i, j: (i, 0)), _weight_spec(w, k, tn)] + side.in_specs,
        out_specs=[pl.BlockSpec((tm, tn), lambda i, j: (i, j))] + side.out_specs,
        out_shape=[jax.ShapeDtypeStruct((t, n), BF16)] + side.out_shapes,
        compiler_params=_params(("arbitrary", "arbitrary"), blocks, _nbytes((tm, tn), F32)),
        name="projection",
    )(a, w.array, *side.inputs())
    return out, side.results(casts)


def _gate_up_kernel(h_ref, wg_ref, wu_ref, *rest, n_side):
    o_ref = rest[n_side]
    h = h_ref[...]
    g = jnp.dot(h, wg_ref[...], preferred_element_type=F32)
    u = jnp.dot(h, wu_ref[...], preferred_element_type=F32)
    o_ref[...] = (g * jax.nn.sigmoid(g) * u).astype(o_ref.dtype)
    _run_side_casts(rest[:n_side], rest[n_side + 1:])


def _gate_up(h, wg, wu, side_weights=()):
    t, k = h.shape
    n = wg.shape[1]
    tm, tn = _tile(t, 1024, 8), _tile(n, 512, 128)
    grid = (t // tm, n // tn)
    side = _SideCasts(side_weights, grid)
    blocks = _nbytes((tm, k), BF16) + 2 * _nbytes((k, tn), BF16) + _nbytes((tm, tn), BF16) + side.block_bytes
    out, *casts = pl.pallas_call(
        functools.partial(_gate_up_kernel, n_side=side.n),
        grid=grid,
        in_specs=[pl.BlockSpec((tm, k), lambda i, j: (i, 0)), _weight_spec(wg, k, tn), _weight_spec(wu, k, tn)]
        + side.in_specs,
        out_specs=[pl.BlockSpec((tm, tn), lambda i, j: (i, j))] + side.out_specs,
        out_shape=[jax.ShapeDtypeStruct((t, n), BF16)] + side.out_shapes,
        compiler_params=_params(("arbitrary", "arbitrary"), blocks, 3 * _nbytes((tm, tn), F32)),
        name="ffn_gate_up",
    )(h, wg.array, wu.array, *side.inputs())
    return out, side.results(casts)


def _residual_kernel(*refs, n_parts, coef):
    a_refs, w_refs = refs[:n_parts], refs[n_parts:2 * n_parts]
    x_ref, gate_ref, o_ref = refs[2 * n_parts:]
    acc = jnp.dot(a_refs[0][...], w_refs[0][...], preferred_element_type=F32)
    for a_ref, w_ref in zip(a_refs[1:], w_refs[1:]):
        acc += jnp.dot(a_ref[...], w_ref[...], preferred_element_type=F32)
    o_ref[...] = x_ref[...] + (coef * gate_ref[0]) * acc


def _residual_matmul(parts, w, x, gate, mod_rows, coef):
    t, n = x.shape
    widths = [p.shape[1] for p in parts]
    assert len(set(widths)) == 1 and sum(widths) == w.shape[0]
    kp = widths[0]
    tm = _tile(mod_rows.span, 1024, 8)
    tn = _tile(n, 512 if _nbytes((tm, sum(widths)), BF16) <= LHS_BLOCK_BUDGET else 256, 128)
    blocks = (len(parts) * (_nbytes((tm, kp), BF16) + _nbytes((kp, tn), BF16))
              + 2 * _nbytes((tm, tn), F32) + _nbytes((1, tn), F32))
    in_specs = [pl.BlockSpec((tm, kp), lambda i, j: (i, 0)) for _ in parts]
    in_specs += [_weight_spec(w, kp, tn, k_block=p) for p in range(len(parts))]
    in_specs += [pl.BlockSpec((tm, tn), lambda i, j: (i, j)),
                 pl.BlockSpec((1, 1, tn), lambda i, j: (mod_rows.of_tile(i, tm), 0, j))]
    return pl.pallas_call(
        functools.partial(_residual_kernel, n_parts=len(parts), coef=coef),
        grid=(t // tm, n // tn),
        in_specs=in_specs,
        out_specs=pl.BlockSpec((tm, tn), lambda i, j: (i, j)),
        out_shape=jax.ShapeDtypeStruct((t, n), F32),
        compiler_params=_params(("parallel", "parallel"), blocks, _nbytes((tm, tn), F32)),
        name="residual_matmul",
    )(*parts, *([w.array] * len(parts)), x, gate)


def _gelu_tanh(x):
    return x * (0.5 * (1.0 + jnp.tanh(math.sqrt(2.0 / math.pi) * (x + 0.044715 * (x * x * x)))))


def _gmlp_kernel(uv_ref, ws_ref, bs_ref, vg_ref, o_ref, *, n_groups, n_chunks):
    for g in range(n_groups):
        u_cols = slice(g * HEAD_DIM, (g + 1) * HEAD_DIM)
        v_cols = slice((n_groups + g) * HEAD_DIM, (n_groups + g + 1) * HEAD_DIM)
        w = ws_ref[g]
        b = bs_ref[g]
        vg = vg_ref[:, u_cols]

        for c in range(n_chunks):
            rows = slice(c * CHUNK, (c + 1) * CHUNK)
            u = _gelu_tanh(uv_ref[rows, u_cols].astype(F32))
            v = _gelu_tanh(uv_ref[rows, v_cols].astype(F32))
            v = v - jnp.mean(v, axis=-1, keepdims=True)
            vn = (v * lax.rsqrt(jnp.mean(v * v, axis=-1, keepdims=True) + NORM_EPS)) * vg
            s = jnp.dot(w, vn.astype(BF16), preferred_element_type=F32) + b
            o_ref[rows, u_cols] = (u * s).astype(o_ref.dtype)


def _gmlp(proj, w_s, b_s, vnorm_g):
    t = proj.shape[0]
    n_groups = w_s.shape[0]
    gm = n_groups * HEAD_DIM
    tc = _tile(t, 512, CHUNK)
    blocks = (_nbytes((tc, 2 * gm), BF16) + _nbytes((tc, gm), BF16) + _nbytes(w_s.shape, BF16)
              + _nbytes((n_groups, CHUNK, 128), F32) + _nbytes((8, gm), F32))
    return pl.pallas_call(
        functools.partial(_gmlp_kernel, n_groups=n_groups, n_chunks=tc // CHUNK),
        grid=(t // tc,),
        in_specs=[pl.BlockSpec((tc, 2 * gm), lambda i: (i, 0)),
                  pl.BlockSpec(w_s.shape, lambda i: (0, 0, 0)),
                  pl.BlockSpec((n_groups, CHUNK, 1), lambda i: (0, 0, 0)),
                  pl.BlockSpec((1, gm), lambda i: (0, 0))],
        out_specs=pl.BlockSpec((tc, gm), lambda i: (i, 0)),
        out_shape=jax.ShapeDtypeStruct((t, gm), BF16),
        compiler_params=_params(("parallel",), blocks, 16 * _nbytes((CHUNK, HEAD_DIM), F32)),
        name="gmlp_chunk_gating",
    )(proj, w_s.astype(BF16), b_s.reshape(n_groups, CHUNK, 1), vnorm_g.reshape(1, gm))


def _swap_rope_halves(x):
    lane = lax.broadcasted_iota(jnp.int32, x.shape, x.ndim - 1)
    return jnp.where(lane % 64 < 32, pltpu.roll(x, HEAD_DIM - 32, x.ndim - 1), pltpu.roll(x, 32, x.ndim - 1))


def _rms_head(x, g):
    x = x.astype(F32)
    return (x * lax.rsqrt(jnp.mean(x * x, axis=-1, keepdims=True) + NORM_EPS)) * g


def _head_norm_kernel(x_ref, g_ref, *rest, rope, heads):
    o_ref = rest[-1]
    for h in range(heads):
        cols = slice(h * HEAD_DIM, (h + 1) * HEAD_DIM)
        y = _rms_head(x_ref[:, cols], g_ref[:, cols])
        if rope:
            cos_ref, sin_ref = rest[:2]
            y = y * cos_ref[...] + _swap_rope_halves(y) * sin_ref[...]
        o_ref[:, cols] = y.astype(o_ref.dtype)


def _head_norm(src, first_head, gains, rope_tables=None):
    t = src.shape[0]
    n_heads = gains.shape[1] // HEAD_DIM
    heads = HEAD_NORM_GROUP if first_head % HEAD_NORM_GROUP == 0 and n_heads % HEAD_NORM_GROUP == 0 else 1
    width = heads * HEAD_DIM
    seq = rope_tables[0].shape[0] if rope_tables is not None else t
    tm = _tile(seq, 1024, 16)
    per_seq = seq // tm
    in_specs = [pl.BlockSpec((tm, width), lambda i, h: (i, first_head // heads + h)),
                pl.BlockSpec((1, width), lambda i, h: (0, h))]
    args = [src, gains]
    if rope_tables is not None:
        in_specs += [pl.BlockSpec((tm, HEAD_DIM), lambda i, h: (i % per_seq, 0))] * 2
        args += list(rope_tables)
    blocks = 2 * _nbytes((tm, width), BF16) + 2 * _nbytes((tm, HEAD_DIM), F32)
    return pl.pallas_call(
        functools.partial(_head_norm_kernel, rope=rope_tables is not None, heads=heads),
        grid=(t // tm, n_heads // heads),
        in_specs=in_specs,
        out_specs=pl.BlockSpec((tm, width), lambda i, h: (i, h)),
        out_shape=jax.ShapeDtypeStruct((t, n_heads * HEAD_DIM), BF16),
        compiler_params=_params(("parallel", "parallel"), blocks, 6 * heads * _nbytes((tm, HEAD_DIM), F32)),
        name="head_norm_rope" if rope_tables is not None else "head_norm",
    )(*args)


def _rope_tables(seq):
    half = HEAD_DIM // 2
    t = jnp.arange(seq)
    rows = (t // GRID_W).astype(F32)
    cols = (t % GRID_W).astype(F32)
    inv = 1.0 / (ROPE_THETA ** (jnp.arange(0, half, 2, dtype=F32) / half))
    ang = jnp.concatenate([rows[:, None] * inv, rows[:, None] * inv, cols[:, None] * inv, cols[:, None] * inv], -1)
    sign = jnp.where(jnp.arange(HEAD_DIM) % half < half // 2, -1.0, 1.0).astype(F32)
    return jnp.cos(ang), jnp.sin(ang) * sign


def _nt_dot(a, b):
    return lax.dot_general(a, b, (((1,), (1,)), ((), ())), preferred_element_type=F32)


def _sink_column(sink_ref, first, group, rows_per_head):
    head = lax.broadcasted_iota(jnp.int32, (group * rows_per_head, 1), 0) // rows_per_head
    col = jnp.zeros((group * rows_per_head, 1), F32)
    for g in range(group):
        col = jnp.where(head == g, sink_ref[first + g], col)
    return col


def _swa_kernel(sink_ref, q_ref, k_ref, v_ref, kc_ref, vc_ref, o_ref, *, tq, seq, group):
    kv_head, q_tile = pl.program_id(1), pl.program_id(2)
    kc, vc = kc_ref[0], vc_ref[0]
    sink = _sink_column(sink_ref, kv_head * group, group, BLOCK)
    q_off = lax.broadcasted_iota(jnp.int32, (group * BLOCK, 1), 0) % BLOCK
    k_off = lax.broadcasted_iota(jnp.int32, (1, 3 * BLOCK), 1)

    def scores(j):
        base = q_tile * tq + j * BLOCK
        start = pl.multiple_of(jnp.clip(base - BLOCK, 0, seq - 3 * BLOCK), BLOCK)
        q_rows = pl.ds(pl.multiple_of(j * BLOCK, BLOCK), BLOCK)
        q = jnp.concatenate([q_ref[0, q_rows, g * HEAD_DIM:(g + 1) * HEAD_DIM] for g in range(group)], axis=0)
        s_w = _nt_dot(q, k_ref[0, pl.ds(start, 3 * BLOCK), :])
        return base, start, q_rows, s_w, _nt_dot(q, kc)

    def attend(base, start, q_rows, s_w, s_c):
        s_w = jnp.where(jnp.abs((start + k_off) - (base + q_off)) <= WINDOW, s_w, NEG_INF)
        m = jnp.maximum(jnp.maximum(jnp.max(s_w, -1, keepdims=True), jnp.max(s_c, -1, keepdims=True)), sink)
        p_w = jnp.exp(s_w - m)
        p_c = jnp.exp(s_c - m)
        den = jnp.sum(p_w, -1, keepdims=True) + jnp.sum(p_c, -1, keepdims=True) + jnp.exp(sink - m)
        o = (jnp.dot(p_w.astype(BF16), v_ref[0, pl.ds(start, 3 * BLOCK), :], preferred_element_type=F32)
             + jnp.dot(p_c.astype(BF16), vc, preferred_element_type=F32)) / den
        for g in range(group):
            o_ref[0, q_rows, g * HEAD_DIM:(g + 1) * HEAD_DIM] = o[g * BLOCK:(g + 1) * BLOCK].astype(o_ref.dtype)

    def block_group(i, carry):
        staged = [scores(i * SWA_BLOCK_UNROLL + u) for u in range(SWA_BLOCK_UNROLL)]
        for block_state in staged:
            attend(*block_state)
        return carry

    assert (tq // BLOCK) % SWA_BLOCK_UNROLL == 0
    lax.fori_loop(0, tq // BLOCK // SWA_BLOCK_UNROLL, block_group, 0)


def _swa(qk, proj, kc, projc, sink, *, n_heads, v_col, batch, seq):
    group = n_heads // SWA_KV_HEADS
    lc = kc.shape[0] // batch
    gw = group * HEAD_DIM
    tq = _tile(seq, 1024, BLOCK)
    qk3 = qk.reshape(batch, seq, qk.shape[1])
    proj3 = proj.reshape(batch, seq, proj.shape[1])
    kc3 = kc.reshape(batch, lc, kc.shape[1])
    projc3 = projc.reshape(batch, lc, projc.shape[1])
    blocks = (2 * _nbytes((tq, gw), BF16) + 2 * _nbytes((seq, HEAD_DIM), BF16) + 2 * _nbytes((lc, HEAD_DIM), BF16))
    temps = 6 * _nbytes((group * BLOCK, 3 * BLOCK + lc), F32)
    out = pl.pallas_call(
        functools.partial(_swa_kernel, tq=tq, seq=seq, group=group),
        grid=(batch, SWA_KV_HEADS, seq // tq),
        in_specs=[pl.BlockSpec(memory_space=pltpu.SMEM),
                  pl.BlockSpec((1, tq, gw), lambda b, h, i: (b, i, h)),
                  pl.BlockSpec((1, seq, HEAD_DIM), lambda b, h, i: (b, 0, n_heads + h)),
                  pl.BlockSpec((1, seq, HEAD_DIM), lambda b, h, i: (b, 0, v_col + h)),
                  pl.BlockSpec((1, lc, HEAD_DIM), lambda b, h, i: (b, 0, h)),
                  pl.BlockSpec((1, lc, HEAD_DIM), lambda b, h, i: (b, 0, v_col + h))],
        out_specs=pl.BlockSpec((1, tq, gw), lambda b, h, i: (b, i, h)),
        out_shape=jax.ShapeDtypeStruct((batch, seq, n_heads * HEAD_DIM), BF16),
        compiler_params=_params(("parallel", "parallel", "parallel"), blocks, temps),
        name="window_attention",
    )(sink, qk3, qk3, proj3, kc3, projc3)
    return out.reshape(batch * seq, n_heads * HEAD_DIM)


def _ctx_attn_kernel(*refs, group, lc, use_sink):
    if use_sink:
        sink_ref, q_ref, k_ref, v_ref, o_ref = refs
    else:
        q_ref, k_ref, v_ref, o_ref = refs
    q = jnp.concatenate([q_ref[0, :, g * HEAD_DIM:(g + 1) * HEAD_DIM] for g in range(group)], axis=0)
    s = _nt_dot(q, k_ref[0])
    m = jnp.max(s, -1, keepdims=True)
    if use_sink:
        sink = _sink_column(sink_ref, pl.program_id(1) * group, group, lc)
        m = jnp.maximum(m, sink)
    p = jnp.exp(s - m)
    den = jnp.sum(p, -1, keepdims=True)
    if use_sink:
        den = den + jnp.exp(sink - m)
    o = jnp.dot(p.astype(BF16), v_ref[0], preferred_element_type=F32) / den
    for g in range(group):
        o_ref[0, :, g * HEAD_DIM:(g + 1) * HEAD_DIM] = o[g * lc:(g + 1) * lc].astype(o_ref.dtype)


def _ctx_attn(qc, kc, projc, sink, *, n_heads, n_kv, v_col, batch):
    group = n_heads // n_kv
    lc = qc.shape[0] // batch
    gw = group * HEAD_DIM
    qc3 = qc.reshape(batch, lc, qc.shape[1])
    kc3 = kc.reshape(batch, lc, kc.shape[1])
    projc3 = projc.reshape(batch, lc, projc.shape[1])
    use_sink = sink is not None
    in_specs = [pl.BlockSpec((1, lc, gw), lambda b, h: (b, 0, h)),
                pl.BlockSpec((1, lc, HEAD_DIM), lambda b, h: (b, 0, h)),
                pl.BlockSpec((1, lc, HEAD_DIM), lambda b, h: (b, 0, v_col + h))]
    args = [qc3, kc3, projc3]
    if use_sink:
        in_specs = [pl.BlockSpec(memory_space=pltpu.SMEM)] + in_specs
        args = [sink] + args
    blocks = 2 * _nbytes((lc, gw), BF16) + 2 * _nbytes((lc, HEAD_DIM), BF16)
    out = pl.pallas_call(
        functools.partial(_ctx_attn_kernel, group=group, lc=lc, use_sink=use_sink),
        grid=(batch, n_kv),
        in_specs=in_specs,
        out_specs=pl.BlockSpec((1, lc, gw), lambda b, h: (b, 0, h)),
        out_shape=jax.ShapeDtypeStruct((batch, lc, n_heads * HEAD_DIM), BF16),
        compiler_params=_params(("parallel", "parallel"), blocks, 6 * _nbytes((group * lc, lc), F32)),
        name="context_attention",
    )(*args)
    return out.reshape(batch * lc, n_heads * HEAD_DIM)


def _nat_bias_kernel(rpb_ref, o_ref):
    h = pl.program_id(0)
    n_dc = 2 * NA_KW - 1
    w = lax.broadcasted_iota(jnp.int32, (GRID_W, 2 * GRID_W), 0)
    lane = lax.broadcasted_iota(jnp.int32, (GRID_W, 2 * GRID_W), 1)
    second = lane >= GRID_W
    kc = lane % GRID_W
    c0 = jnp.clip(w - NA_KW // 2, 0, GRID_W - NA_KW)
    dc = jnp.where((kc >= c0) & (kc < c0 + NA_KW), kc - w + (NA_KW - 1), -1)

    def pair(dr, carry):
        base = (h * (2 * NA_KH - 1) + dr) * n_dc
        tile = jnp.full((GRID_W, 2 * GRID_W), NEG_INF, F32)
        for j in range(n_dc):
            tile = jnp.where(dc == j, jnp.where(second, rpb_ref[base + n_dc + j], rpb_ref[base + j]), tile)
        o_ref[0, dr] = tile
        return carry

    lax.fori_loop(0, 2 * NA_KH - 2, pair, 0)


def _nat_bias(rpb):
    n_heads = rpb.shape[0]
    assert rpb.shape[1:] == (2 * NA_KH - 1, 2 * NA_KW - 1)
    shape = (n_heads, 2 * NA_KH - 2, GRID_W, 2 * GRID_W)
    return pl.pallas_call(
        _nat_bias_kernel,
        grid=(n_heads,),
        in_specs=[pl.BlockSpec(memory_space=pltpu.SMEM)],
        out_specs=pl.BlockSpec((1,) + shape[1:], lambda h: (h, 0, 0, 0)),
        out_shape=jax.ShapeDtypeStruct(shape, F32),
        compiler_params=_params(("parallel",), _nbytes(shape[1:], F32)),
        name="nat_bias_tiles",
    )(rpb.reshape(-1))


def _nat_kernel(q_ref, k_ref, v_ref, kc_ref, vc_ref, gq_ref, gk_ref, bias_ref, o_ref, kn_ref, *, rows, norm_rows,
                heads):
    n_pairs = NA_KH // 2
    gq = gq_ref[...] * ATTN_SCALE
    gk = gk_ref[...]
    n_ctx = kc_ref.shape[1] // HEAD_DIM

    for head in range(heads):
        cols = slice(head * HEAD_DIM, (head + 1) * HEAD_DIM)

        def norm_keys(i, carry, cols=cols):
            chunk = pl.ds(pl.multiple_of(i * norm_rows, norm_rows), norm_rows)
            kn_ref[chunk, :] = _rms_head(k_ref[0, chunk, cols], gk).astype(BF16)
            return carry

        lax.fori_loop(0, rows * GRID_W // norm_rows, norm_keys, 0)
        kc = _rms_head(kc_ref[0, :, cols], gk).astype(BF16)
        vc = vc_ref[0, :, cols]

        def scores(r, cols=cols, kc=kc):
            r0 = jnp.clip(r - NA_KH // 2, 0, rows - NA_KH)
            dr0 = r0 - r + (NA_KH - 1)
            q_rows = pl.ds(pl.multiple_of(r * GRID_W, GRID_W), GRID_W)
            k_rows = pl.ds(pl.multiple_of(r0 * GRID_W, GRID_W), NA_KH * GRID_W)
            q = _rms_head(q_ref[0, q_rows, cols], gq).astype(BF16)
            return q_rows, k_rows, dr0, _nt_dot(q, kn_ref[k_rows, :]), _nt_dot(q, kc)

        def attend(q_rows, k_rows, dr0, s_nb, s_c, cols=cols, vc=vc, head=head):
            s = [s_nb[:, 2 * GRID_W * j:2 * GRID_W * (j + 1)] + bias_ref[head, dr0 + 2 * j] for j in range(n_pairs)]
            s += [s_c[:, HEAD_DIM * j:HEAD_DIM * (j + 1)] for j in range(n_ctx)]
            m = jnp.max(functools.reduce(jnp.maximum, s), -1, keepdims=True)
            p = [jnp.exp(sj - m) for sj in s]
            den = jnp.sum(functools.reduce(jnp.add, p), -1, keepdims=True)
            p_nb = jnp.concatenate(p[:n_pairs], axis=-1).astype(BF16)
            p_c = jnp.concatenate(p[n_pairs:], axis=-1).astype(BF16)
            o = (jnp.dot(p_nb, v_ref[0, k_rows, cols], preferred_element_type=F32)
                 + jnp.dot(p_c, vc, preferred_element_type=F32)) / den
            o_ref[0, q_rows, cols] = o.astype(o_ref.dtype)

        def row_group(i, carry, scores=scores, attend=attend):
            staged = [scores(i * NAT_ROW_UNROLL + u) for u in range(NAT_ROW_UNROLL)]
            for row_state in staged:
                attend(*row_state)
            return carry

        lax.fori_loop(0, rows // NAT_ROW_UNROLL, row_group, 0)


def _nat(proj, projc, gq, gk, bias, *, n_heads, batch, seq):
    rows = seq // GRID_W
    assert seq % GRID_W == 0 and rows >= NA_KH and rows % NAT_ROW_UNROLL == 0
    lc = projc.shape[0] // batch
    assert lc % HEAD_DIM == 0
    heads = NAT_HEADS_PER_STEP if n_heads % NAT_HEADS_PER_STEP == 0 else 1
    width = heads * HEAD_DIM
    norm_rows = _tile(seq, 512, 16)
    proj3 = proj.reshape(batch, seq, proj.shape[1])
    projc3 = projc.reshape(batch, lc, projc.shape[1])

    def head_block(n, first_head):
        return pl.BlockSpec((1, n, width), lambda b, h: (b, 0, first_head // heads + h))

    gain_block = pl.BlockSpec((1, HEAD_DIM), lambda b, h: (0, 0))
    bias_block = (heads,) + bias.shape[1:]
    blocks = 4 * _nbytes((seq, width), BF16) + 2 * _nbytes((lc, width), BF16) + _nbytes(bias_block, F32)
    temps = (_nbytes((seq, HEAD_DIM), BF16)
             + 4 * NAT_ROW_UNROLL * _nbytes((GRID_W, NA_KH * GRID_W + lc), F32) + 4 * _nbytes((norm_rows, HEAD_DIM), F32))
    out = pl.pallas_call(
        functools.partial(_nat_kernel, rows=rows, norm_rows=norm_rows, heads=heads),
        grid=(batch, n_heads // heads),
        in_specs=[head_block(seq, 0), head_block(seq, n_heads), head_block(seq, 2 * n_heads),
                  head_block(lc, n_heads), head_block(lc, 2 * n_heads),
                  gain_block, gain_block,
                  pl.BlockSpec(bias_block, lambda b, h: (h, 0, 0, 0))],
        out_specs=head_block(seq, 0),
        out_shape=jax.ShapeDtypeStruct((batch, seq, n_heads * HEAD_DIM), BF16),
        scratch_shapes=[pltpu.VMEM((seq, HEAD_DIM), BF16)],
        compiler_params=_params(("parallel", "parallel"), blocks, temps),
        name="neighbourhood_attention",
    )(proj3, proj3, proj3, projc3, projc3, gq.reshape(1, HEAD_DIM), gk.reshape(1, HEAD_DIM), bias)
    return out.reshape(batch * seq, n_heads * HEAD_DIM)


def _gains(*groups):
    return jnp.concatenate([jnp.tile(g.astype(F32), n) for g, n in groups]).reshape(1, -1)


def kernel(x, c, ctx, c_ctx, w_ada, b_ada, norm_g, ffn_w_gate, ffn_w_up, ffn_w_down, mix_ab_w_in, mix_ab_w_out,
           gm_vnorm_g, gm_w_s, gm_b_s, swa_qnorm_g, swa_knorm_g, swa_sink, nat_w_qkv, nat_w_out, nat_qnorm_g,
           nat_knorm_g, nat_rpb):
    batch, seq, d = x.shape
    lc = ctx.shape[1]
    depth = w_ada.shape[0]
    ctx_row = batch
    assert batch + 1 <= MOD_ROWS

    cond = jnp.zeros((MOD_ROWS, d), F32).at[:batch].set(c).at[ctx_row].set(c_ctx)
    mod = _ada(cond, w_ada, b_ada).reshape(depth, MOD_ROWS, N_MOD, 1, d)

    lat_row = _ModRows(0, seq)
    ctx_row_of = _ModRows(ctx_row, batch * lc)

    def mod_vec(layer, sub, which):
        return mod[layer, :, 3 * sub + which]

    def ffn_in(xs, row, layer, sub):
        return _norm_mod(xs, norm_g[layer, sub], mod_vec(layer, sub, 0), mod_vec(layer, sub, 1), row)

    def ffn_out(a, wd, xs, row, layer, sub):
        return _residual_matmul([a], wd, xs, mod_vec(layer, sub, 2), row, HALF_STEP)

    def ffn_ctx(cs, layer, sub, wg, wu, wd):
        a, _ = _gate_up(ffn_in(cs, ctx_row_of, layer, sub), wg, wu)
        return ffn_out(a, wd, cs, ctx_row_of, layer, sub)

    def ffn_f32(layer, idx):
        return (_Weight(w, (layer, idx)) for w in (ffn_w_gate, ffn_w_up, ffn_w_down))

    xs = x.reshape(batch * seq, d)
    cs = ctx.reshape(batch * lc, d)
    rope = _rope_tables(seq)

    wg0_f32, wu0_f32, _ = ffn_f32(0, 0)
    wg, wu = (_Weight(w.array[w.lead].astype(BF16)) for w in (wg0_f32, wu0_f32))

    for layer in range(depth):
        last = layer == depth - 1
        j = layer // 2
        even = layer % 2 == 0
        mix_in_f32 = _Weight(mix_ab_w_in if even else nat_w_qkv, (j,))
        mix_out_f32 = _Weight(mix_ab_w_out if even else nat_w_out, (j,))
        _, _, wd_f32 = ffn_f32(layer, 0)
        a, (wd, mix_in, mix_out) = _gate_up(ffn_in(xs, lat_row, layer, 0), wg, wu, [wd_f32, mix_in_f32, mix_out_f32])
        xs = ffn_out(a, wd, xs, lat_row, layer, 0)
        cs = ffn_ctx(cs, layer, 0, wg, wu, wd)
        h = _norm_mod(xs, norm_g[layer, 1], mod_vec(layer, 1, 0), mod_vec(layer, 1, 1), lat_row)
        hc = _norm_mod(cs, norm_g[layer, 1], mod_vec(layer, 1, 0), mod_vec(layer, 1, 1), ctx_row_of)
        gate = mod_vec(layer, 1, 2)
        wg1_f32, wu1_f32, wd1_f32 = ffn_f32(layer, 1)
        proj, (wg, wu) = _matmul(h, mix_in, [wg1_f32, wu1_f32])
        projc, _ = _matmul(hc, mix_in)
        if even:
            n_groups = gm_w_s.shape[1]
            n_heads = swa_sink.shape[1]
            assert n_groups == n_heads
            q_col = 2 * n_groups
            v_col = q_col + n_heads + SWA_KV_HEADS
            qk_gain = _gains((swa_qnorm_g[j] * ATTN_SCALE, n_heads), (swa_knorm_g[j], SWA_KV_HEADS))
            qk = _head_norm(proj, q_col, qk_gain, rope)
            qkc = _head_norm(projc, q_col, qk_gain)
            kc = qkc[:, n_heads * HEAD_DIM:]
            y_g = _gmlp(proj, gm_w_s[j], gm_b_s[j], gm_vnorm_g[j])
            y_a = _swa(qk, proj, kc, projc, swa_sink[j], n_heads=n_heads, v_col=v_col, batch=batch, seq=seq)
            xs = _residual_matmul([y_g, y_a], mix_out, xs, gate, lat_row, 1.0)
            if not last:
                yc_g = _gmlp(projc, gm_w_s[j], gm_b_s[j], gm_vnorm_g[j])
                yc_a = _ctx_attn(qkc, kc, projc, swa_sink[j], n_heads=n_heads, n_kv=SWA_KV_HEADS, v_col=v_col,
                                 batch=batch)
                cs = _residual_matmul([yc_g, yc_a], mix_out, cs, gate, ctx_row_of, 1.0)
        else:
            n_heads = nat_rpb.shape[1]
            y = _nat(proj, projc, nat_qnorm_g[j], nat_knorm_g[j], _nat_bias(nat_rpb[j]), n_heads=n_heads,
                     batch=batch, seq=seq)
            xs = _residual_matmul([y], mix_out, xs, gate, lat_row, 1.0)
            if not last:
                qk_gain = _gains((nat_qnorm_g[j] * ATTN_SCALE, n_heads), (nat_knorm_g[j], n_heads))
                qkc = _head_norm(projc, 0, qk_gain)
                yc = _ctx_attn(qkc, qkc[:, n_heads * HEAD_DIM:], projc, None, n_heads=n_heads, n_kv=n_heads,
                               v_col=2 * n_heads, batch=batch)
                cs = _residual_matmul([yc], mix_out, cs, gate, ctx_row_of, 1.0)
        next_f32 = [] if last else list(ffn_f32(layer + 1, 0))[:2]
        a, (wd1, *next_w) = _gate_up(ffn_in(xs, lat_row, layer, 2), wg, wu, [wd1_f32] + next_f32)
        xs = ffn_out(a, wd1, xs, lat_row, layer, 2)
        if not last:
            cs = ffn_ctx(cs, layer, 2, wg, wu, wd1)
            wg, wu = next_w
    return xs.reshape(batch, seq, d)
```

```python
import functools
import math
from typing import NamedTuple

import jax
import jax.numpy as jnp
from jax import lax
from jax.experimental import pallas as pl
from jax.experimental.pallas import tpu as pltpu

F32 = jnp.float32
BF16 = jnp.bfloat16

HEAD_DIM = 128
GRID_W = 64
CHUNK = 128
WINDOW = 128
BLOCK = 128
SWA_KV_HEADS = 4
NA_KH = 8
NA_KW = 16
NAT_ROW_UNROLL = 8
NAT_HEADS_PER_STEP = 1
SWA_BLOCK_UNROLL = 2
HEAD_NORM_GROUP = 4
N_MOD = 9
HALF_STEP = 0.5
ROPE_THETA = 10000.0
NORM_EPS = 1e-6
NEG_INF = -1e30
ATTN_SCALE = HEAD_DIM ** -0.5

V7X_VMEM_BYTES = 64 * 1024 * 1024
VMEM_CEILING = V7X_VMEM_BYTES - 8 * 1024 * 1024
MOD_ROWS = 8
BF16_SUBLANES = 16
STREAM_SPLITS = 4
MXU_WIDTH = 256
NORM_ROW_GROUP = BF16_SUBLANES
NORM_GROUP_UNROLL = 4
LHS_BLOCK_BUDGET = 8 * 1024 * 1024


def _vmem_limit(pipelined_bytes, temp_bytes=0):
    need = int((2 * pipelined_bytes + temp_bytes) * 1.25) + (2 << 20)
    return min(max(need, 16 << 20), VMEM_CEILING)


def _params(semantics, pipelined_bytes, temp_bytes=0):
    return pltpu.CompilerParams(dimension_semantics=semantics,
                                vmem_limit_bytes=_vmem_limit(pipelined_bytes, temp_bytes))


def _tile(n, pref, quantum):
    if n <= pref:
        return n
    t = (pref // quantum) * quantum
    while t > quantum and n % t:
        t -= quantum
    assert n % t == 0, (n, pref, quantum)
    return t


def _nbytes(shape, dtype):
    return math.prod(shape) * jnp.dtype(dtype).itemsize


def _ada_kernel(c_ref, *refs):
    w_refs, (b_ref, o_ref) = refs[:-2], refs[-2:]
    c = c_ref[...]
    s = (c * jax.nn.sigmoid(c)).astype(BF16)
    k = w_refs[0].shape[1]
    acc = b_ref[0]
    for part, w_ref in enumerate(w_refs):
        acc = acc + jnp.dot(s[:, part * k:(part + 1) * k], w_ref[0].astype(BF16), preferred_element_type=F32)
    o_ref[0] = acc


def _ada(cond, w_ada, b_ada):
    depth, d, n = w_ada.shape
    tn = _tile(n, 512, 128)
    parts = STREAM_SPLITS if d % (STREAM_SPLITS * HEAD_DIM) == 0 else 1
    k = d // parts
    blocks = _nbytes((MOD_ROWS, d), F32) + _nbytes((d, tn), F32) + 2 * _nbytes((MOD_ROWS, tn), F32)
    w_specs = [pl.BlockSpec((1, k, tn), functools.partial(lambda l, j, p: (l, p, j), p=p)) for p in range(parts)]
    return pl.pallas_call(
        _ada_kernel,
        grid=(depth, n // tn),
        in_specs=[pl.BlockSpec((MOD_ROWS, d), lambda l, j: (0, 0))] + w_specs
        + [pl.BlockSpec((1, 1, tn), lambda l, j: (l, 0, j))],
        out_specs=pl.BlockSpec((1, MOD_ROWS, tn), lambda l, j: (l, 0, j)),
        out_shape=jax.ShapeDtypeStruct((depth, MOD_ROWS, n), F32),
        compiler_params=_params(("parallel", "parallel"), blocks, _nbytes((d, tn), BF16)),
        name="ada_modulation",
    )(cond, *([w_ada] * parts), b_ada.reshape(depth, 1, n))


def _norm_mod_kernel(*refs):
    x_refs, (g_ref, shift_ref, scale_ref, o_ref) = refs[:-4], refs[-4:]
    gain = g_ref[...] * (1.0 + scale_ref[0])
    shift = shift_ref[0]
    part_rows = x_refs[0].shape[0]

    for part, x_ref in enumerate(x_refs):
        def row_group(i, carry, x_ref=x_ref, part=part):
            start = pl.multiple_of(i * NORM_ROW_GROUP, NORM_ROW_GROUP)
            x = x_ref[pl.ds(start, NORM_ROW_GROUP), :]
            y = x * lax.rsqrt(jnp.mean(x * x, axis=-1, keepdims=True) + NORM_EPS)
            o_ref[pl.ds(part * part_rows + start, NORM_ROW_GROUP), :] = (y * gain + shift).astype(o_ref.dtype)
            return carry

        lax.fori_loop(0, part_rows // NORM_ROW_GROUP, row_group, 0, unroll=NORM_GROUP_UNROLL)


class _ModRows:
    def __init__(self, first, span):
        self.first, self.span = first, span

    def of_tile(self, i, tm):
        assert self.span % tm == 0
        return self.first + (i * tm) // self.span


def _norm_mod(x, g, shift, scale, mod_rows):
    t, d = x.shape
    tm = _tile(mod_rows.span, 256, NORM_ROW_GROUP)
    parts = STREAM_SPLITS if tm % (STREAM_SPLITS * NORM_ROW_GROUP * NORM_GROUP_UNROLL) == 0 else 1
    mod_spec = pl.BlockSpec((1, 1, d), lambda i: (mod_rows.of_tile(i, tm), 0, 0))
    blocks = _nbytes((tm, d), F32) + _nbytes((tm, d), BF16) + 3 * _nbytes((1, d), F32)
    x_specs = [pl.BlockSpec((tm // parts, d), functools.partial(lambda i, p: (i * parts + p, 0), p=p))
               for p in range(parts)]
    return pl.pallas_call(
        _norm_mod_kernel,
        grid=(t // tm,),
        in_specs=x_specs + [pl.BlockSpec((1, d), lambda i: (0, 0)), mod_spec, mod_spec],
        out_specs=pl.BlockSpec((tm, d), lambda i: (i, 0)),
        out_shape=jax.ShapeDtypeStruct((t, d), BF16),
        compiler_params=_params(("parallel",), blocks, 2 * _nbytes((tm, d), F32)),
        name="norm_modulate",
    )(*([x] * parts), g.reshape(1, d), shift, scale)


class _Weight(NamedTuple):
    array: jax.Array
    lead: tuple = ()

    @property
    def shape(self):
        return self.array.shape[-2:]


def _weight_spec(w, k, tn, k_block=0):
    return pl.BlockSpec((None,) * len(w.lead) + (k, tn), lambda i, j: w.lead + (k_block, j))


def _cast_rows(rows, n_steps):
    for r in range(BF16_SUBLANES, rows + 1, BF16_SUBLANES):
        if rows % r == 0 and rows // r <= n_steps:
            return r
    return None


class _SideCasts:
    def __init__(self, weights, grid):
        n_steps, n_inner = grid[0] * grid[1], grid[1]
        self.weights = list(weights)
        self.rows = [_cast_rows(w.shape[0], n_steps) for w in self.weights]
        self.carried = [w for w, r in zip(self.weights, self.rows) if r is not None]
        self.in_specs, self.out_specs, self.out_shapes, self.block_bytes = [], [], [], 0
        for w, r in zip(self.weights, self.rows):
            if r is None:
                continue
            rows, cols = w.shape
            block = functools.partial(lambda i, j, last: jnp.minimum(i * n_inner + j, last), last=rows // r - 1)
            self.in_specs.append(pl.BlockSpec((None,) * len(w.lead) + (r, cols),
                                              functools.partial(lambda i, j, w, block: w.lead + (block(i, j), 0),
                                                                w=w, block=block)))
            self.out_specs.append(pl.BlockSpec((r, cols), functools.partial(lambda i, j, block: (block(i, j), 0),
                                                                            block=block)))
            self.out_shapes.append(jax.ShapeDtypeStruct((rows, cols), BF16))
            self.block_bytes += _nbytes((r, cols), F32) + _nbytes((r, cols), BF16)

    @property
    def n(self):
        return len(self.carried)

    def inputs(self):
        return [w.array for w in self.carried]

    def results(self, outs):
        outs = iter(outs)
        return [_Weight(next(outs)) if r is not None else _Weight(w.array[w.lead].astype(BF16))
                for w, r in zip(self.weights, self.rows)]


def _column_halves(n):
    if n % (2 * MXU_WIDTH):
        return (slice(0, n),)
    return (slice(0, n // 2), slice(n // 2, n))


def _run_side_casts(in_refs, out_refs):
    for src, dst in zip(in_refs, out_refs):
        dst[...] = src[...].astype(dst.dtype)


def _matmul_kernel(a_ref, w_ref, *rest, n_side):
    o_ref = rest[n_side]
    _run_side_casts(rest[:n_side], rest[n_side + 1:])
    for cols in _column_halves(o_ref.shape[1]):
        o_ref[:, cols] = jnp.dot(a_ref[...], w_ref[:, cols], preferred_element_type=F32).astype(o_ref.dtype)


def _matmul(a, w, side_weights=()):
    t, k = a.shape
    n = w.shape[1]
    tm, tn = _tile(t, 1024, 8), _tile(n, 512, 128)
    grid = (t // tm, n // tn)
    side = _SideCasts(side_weights, grid)
    blocks = _nbytes((tm, k), BF16) + _nbytes((k, tn), BF16) + _nbytes((tm, tn), BF16) + side.block_bytes
    out, *casts = pl.pallas_call(
        functools.partial(_matmul_kernel, n_side=side.n),
        grid=grid,
        in_specs=[pl.BlockSpec((tm, k), lambda i, j: (i, 0)), _weight_spec(w, k, tn)] + side.in_specs,
        out_specs=[pl.BlockSpec((tm, tn), lambda i, j: (i, j))] + side.out_specs,
        out_shape=[jax.ShapeDtypeStruct((t, n), BF16)] + side.out_shapes,
        compiler_params=_params(("arbitrary", "arbitrary"), blocks, _nbytes((tm, tn), F32)),
        name="projection",
    )(a, w.array, *side.inputs())
    return out, side.results(casts)


def _gate_up_kernel(h_ref, wg_ref, wu_ref, *rest, n_side):
    o_ref = rest[n_side]
    _run_side_casts(rest[:n_side], rest[n_side + 1:])
    h = h_ref[...]
    for cols in _column_halves(o_ref.shape[1]):
        g = jnp.dot(h, wg_ref[:, cols], preferred_element_type=F32)
        u = jnp.dot(h, wu_ref[:, cols], preferred_element_type=F32)
        o_ref[:, cols] = (g * jax.nn.sigmoid(g) * u).astype(o_ref.dtype)


def _gate_up(h, wg, wu, side_weights=()):
    t, k = h.shape
    n = wg.shape[1]
    tm, tn = _tile(t, 1024, 8), _tile(n, 512, 128)
    grid = (t // tm, n // tn)
    side = _SideCasts(side_weights, grid)
    blocks = _nbytes((tm, k), BF16) + 2 * _nbytes((k, tn), BF16) + _nbytes((tm, tn), BF16) + side.block_bytes
    out, *casts = pl.pallas_call(
        functools.partial(_gate_up_kernel, n_side=side.n),
        grid=grid,
        in_specs=[pl.BlockSpec((tm, k), lambda i, j: (i, 0)), _weight_spec(wg, k, tn), _weight_spec(wu, k, tn)]
        + side.in_specs,
        out_specs=[pl.BlockSpec((tm, tn), lambda i, j: (i, j))] + side.out_specs,
        out_shape=[jax.ShapeDtypeStruct((t, n), BF16)] + side.out_shapes,
        compiler_params=_params(("arbitrary", "arbitrary"), blocks, 3 * _nbytes((tm, tn), F32)),
        name="ffn_gate_up",
    )(h, wg.array, wu.array, *side.inputs())
    return out, side.results(casts)


def _residual_kernel(*refs, n_parts, coef):
    a_refs, w_refs = refs[:n_parts], refs[n_parts:2 * n_parts]
    x_ref, gate_ref, o_ref = refs[2 * n_parts:]
    for cols in _column_halves(o_ref.shape[1]):
        acc = jnp.dot(a_refs[0][...], w_refs[0][:, cols], preferred_element_type=F32)
        for a_ref, w_ref in zip(a_refs[1:], w_refs[1:]):
            acc += jnp.dot(a_ref[...], w_ref[:, cols], preferred_element_type=F32)
        o_ref[:, cols] = x_ref[:, cols] + (coef * gate_ref[0, :, cols]) * acc


def _residual_matmul(parts, w, x, gate, mod_rows, coef):
    t, n = x.shape
    widths = [p.shape[1] for p in parts]
    assert len(set(widths)) == 1 and sum(widths) == w.shape[0]
    kp = widths[0]
    tm = _tile(mod_rows.span, 1024 if _nbytes((1024, sum(widths)), BF16) <= LHS_BLOCK_BUDGET else 512, 8)
    tn = _tile(n, 512, 128)
    blocks = (len(parts) * (_nbytes((tm, kp), BF16) + _nbytes((kp, tn), BF16))
              + 2 * _nbytes((tm, tn), F32) + _nbytes((1, tn), F32))
    in_specs = [pl.BlockSpec((tm, kp), lambda i, j: (i, 0)) for _ in parts]
    in_specs += [_weight_spec(w, kp, tn, k_block=p) for p in range(len(parts))]
    in_specs += [pl.BlockSpec((tm, tn), lambda i, j: (i, j)),
                 pl.BlockSpec((1, 1, tn), lambda i, j: (mod_rows.of_tile(i, tm), 0, j))]
    return pl.pallas_call(
        functools.partial(_residual_kernel, n_parts=len(parts), coef=coef),
        grid=(t // tm, n // tn),
        in_specs=in_specs,
        out_specs=pl.BlockSpec((tm, tn), lambda i, j: (i, j)),
        out_shape=jax.ShapeDtypeStruct((t, n), F32),
        compiler_params=_params(("parallel", "parallel"), blocks, _nbytes((tm, tn), F32)),
        name="residual_matmul",
    )(*parts, *([w.array] * len(parts)), x, gate)


def _gelu_tanh(x):
    return x * (0.5 * (1.0 + jnp.tanh(math.sqrt(2.0 / math.pi) * (x + 0.044715 * (x * x * x)))))


def _gmlp_kernel(uv_ref, ws_ref, bs_ref, vg_ref, o_ref, *, n_groups, n_chunks):
    for g in range(n_groups):
        u_cols = slice(g * HEAD_DIM, (g + 1) * HEAD_DIM)
        v_cols = slice((n_groups + g) * HEAD_DIM, (n_groups + g + 1) * HEAD_DIM)
        w = ws_ref[g]
        b = bs_ref[g]
        vg = vg_ref[:, u_cols]

        for c in range(n_chunks):
            rows = slice(c * CHUNK, (c + 1) * CHUNK)
            u = _gelu_tanh(uv_ref[rows, u_cols].astype(F32))
            v = _gelu_tanh(uv_ref[rows, v_cols].astype(F32))
            v = v - jnp.mean(v, axis=-1, keepdims=True)
            vn = (v * lax.rsqrt(jnp.mean(v * v, axis=-1, keepdims=True) + NORM_EPS)) * vg
            s = jnp.dot(w, vn.astype(BF16), preferred_element_type=F32) + b
            o_ref[rows, u_cols] = (u * s).astype(o_ref.dtype)


def _gmlp(proj, w_s, b_s, vnorm_g):
    t = proj.shape[0]
    n_groups = w_s.shape[0]
    gm = n_groups * HEAD_DIM
    tc = _tile(t, 512, CHUNK)
    blocks = (_nbytes((tc, 2 * gm), BF16) + _nbytes((tc, gm), BF16) + _nbytes(w_s.shape, BF16)
              + _nbytes((n_groups, CHUNK, 128), F32) + _nbytes((8, gm), F32))
    return pl.pallas_call(
        functools.partial(_gmlp_kernel, n_groups=n_groups, n_chunks=tc // CHUNK),
        grid=(t // tc,),
        in_specs=[pl.BlockSpec((tc, 2 * gm), lambda i: (i, 0)),
                  pl.BlockSpec(w_s.shape, lambda i: (0, 0, 0)),
                  pl.BlockSpec((n_groups, CHUNK, 1), lambda i: (0, 0, 0)),
                  pl.BlockSpec((1, gm), lambda i: (0, 0))],
        out_specs=pl.BlockSpec((tc, gm), lambda i: (i, 0)),
        out_shape=jax.ShapeDtypeStruct((t, gm), BF16),
        compiler_params=_params(("parallel",), blocks, 16 * _nbytes((CHUNK, HEAD_DIM), F32)),
        name="gmlp_chunk_gating",
    )(proj, w_s.astype(BF16), b_s.reshape(n_groups, CHUNK, 1), vnorm_g.reshape(1, gm))


def _swap_rope_halves(x):
    lane = lax.broadcasted_iota(jnp.int32, x.shape, x.ndim - 1)
    return jnp.where(lane % 64 < 32, pltpu.roll(x, HEAD_DIM - 32, x.ndim - 1), pltpu.roll(x, 32, x.ndim - 1))


def _rms_head(x, g):
    x = x.astype(F32)
    return (x * lax.rsqrt(jnp.mean(x * x, axis=-1, keepdims=True) + NORM_EPS)) * g


def _head_norm_kernel(x_ref, g_ref, *rest, rope, heads):
    o_ref = rest[-1]
    for h in range(heads):
        cols = slice(h * HEAD_DIM, (h + 1) * HEAD_DIM)
        y = _rms_head(x_ref[:, cols], g_ref[:, cols])
        if rope:
            cos_ref, sin_ref = rest[:2]
            y = y * cos_ref[...] + _swap_rope_halves(y) * sin_ref[...]
        o_ref[:, cols] = y.astype(o_ref.dtype)


def _head_norm(src, first_head, gains, rope_tables=None):
    t = src.shape[0]
    n_heads = gains.shape[1] // HEAD_DIM
    heads = HEAD_NORM_GROUP if first_head % HEAD_NORM_GROUP == 0 and n_heads % HEAD_NORM_GROUP == 0 else 1
    width = heads * HEAD_DIM
    seq = rope_tables[0].shape[0] if rope_tables is not None else t
    tm = _tile(seq, 1024, 16)
    per_seq = seq // tm
    in_specs = [pl.BlockSpec((tm, width), lambda i, h: (i, first_head // heads + h)),
                pl.BlockSpec((1, width), lambda i, h: (0, h))]
    args = [src, gains]
    if rope_tables is not None:
        in_specs += [pl.BlockSpec((tm, HEAD_DIM), lambda i, h: (i % per_seq, 0))] * 2
        args += list(rope_tables)
    blocks = 2 * _nbytes((tm, width), BF16) + 2 * _nbytes((tm, HEAD_DIM), F32)
    return pl.pallas_call(
        functools.partial(_head_norm_kernel, rope=rope_tables is not None, heads=heads),
        grid=(t // tm, n_heads // heads),
        in_specs=in_specs,
        out_specs=pl.BlockSpec((tm, width), lambda i, h: (i, h)),
        out_shape=jax.ShapeDtypeStruct((t, n_heads * HEAD_DIM), BF16),
        compiler_params=_params(("parallel", "parallel"), blocks, 6 * heads * _nbytes((tm, HEAD_DIM), F32)),
        name="head_norm_rope" if rope_tables is not None else "head_norm",
    )(*args)


def _rope_tables(seq):
    half = HEAD_DIM // 2
    t = jnp.arange(seq)
    rows = (t // GRID_W).astype(F32)
    cols = (t % GRID_W).astype(F32)
    inv = 1.0 / (ROPE_THETA ** (jnp.arange(0, half, 2, dtype=F32) / half))
    ang = jnp.concatenate([rows[:, None] * inv, rows[:, None] * inv, cols[:, None] * inv, cols[:, None] * inv], -1)
    sign = jnp.where(jnp.arange(HEAD_DIM) % half < half // 2, -1.0, 1.0).astype(F32)
    return jnp.cos(ang), jnp.sin(ang) * sign


def _nt_dot(a, b):
    return lax.dot_general(a, b, (((1,), (1,)), ((), ())), preferred_element_type=F32)


def _sink_column(sink_ref, first, group, rows_per_head):
    head = lax.broadcasted_iota(jnp.int32, (group * rows_per_head, 1), 0) // rows_per_head
    col = jnp.zeros((group * rows_per_head, 1), F32)
    for g in range(group):
        col = jnp.where(head == g, sink_ref[first + g], col)
    return col


def _swa_kernel(sink_ref, q_ref, k_ref, v_ref, kc_ref, vc_ref, o_ref, *, tq, seq, group):
    kv_head, q_tile = pl.program_id(1), pl.program_id(2)
    kc, vc = kc_ref[0], vc_ref[0]
    sink = _sink_column(sink_ref, kv_head * group, group, BLOCK)
    q_off = lax.broadcasted_iota(jnp.int32, (group * BLOCK, 1), 0) % BLOCK
    k_off = lax.broadcasted_iota(jnp.int32, (1, 3 * BLOCK), 1)

    def scores(j):
        base = q_tile * tq + j * BLOCK
        start = pl.multiple_of(jnp.clip(base - BLOCK, 0, seq - 3 * BLOCK), BLOCK)
        q_rows = pl.ds(pl.multiple_of(j * BLOCK, BLOCK), BLOCK)
        q = jnp.concatenate([q_ref[0, q_rows, g * HEAD_DIM:(g + 1) * HEAD_DIM] for g in range(group)], axis=0)
        s_w = _nt_dot(q, k_ref[0, pl.ds(start, 3 * BLOCK), :])
        return base, start, q_rows, s_w, _nt_dot(q, kc)

    def attend(base, start, q_rows, s_w, s_c):
        s_w = jnp.where(jnp.abs((start + k_off) - (base + q_off)) <= WINDOW, s_w, NEG_INF)
        m = jnp.maximum(jnp.maximum(jnp.max(s_w, -1, keepdims=True), jnp.max(s_c, -1, keepdims=True)), sink)
        p_w = jnp.exp(s_w - m)
        p_c = jnp.exp(s_c - m)
        den = jnp.sum(p_w, -1, keepdims=True) + jnp.sum(p_c, -1, keepdims=True) + jnp.exp(sink - m)
        o = (jnp.dot(p_w.astype(BF16), v_ref[0, pl.ds(start, 3 * BLOCK), :], preferred_element_type=F32)
             + jnp.dot(p_c.astype(BF16), vc, preferred_element_type=F32)) / den
        for g in range(group):
            o_ref[0, q_rows, g * HEAD_DIM:(g + 1) * HEAD_DIM] = o[g * BLOCK:(g + 1) * BLOCK].astype(o_ref.dtype)

    def block_group(i, carry):
        staged = [scores(i * SWA_BLOCK_UNROLL + u) for u in range(SWA_BLOCK_UNROLL)]
        for block_state in staged:
            attend(*block_state)
        return carry

    assert (tq // BLOCK) % SWA_BLOCK_UNROLL == 0
    lax.fori_loop(0, tq // BLOCK // SWA_BLOCK_UNROLL, block_group, 0)


def _swa(qk, proj, kc, projc, sink, *, n_heads, v_col, batch, seq):
    group = n_heads // SWA_KV_HEADS
    lc = kc.shape[0] // batch
    gw = group * HEAD_DIM
    tq = _tile(seq, 1024, BLOCK)
    qk3 = qk.reshape(batch, seq, qk.shape[1])
    proj3 = proj.reshape(batch, seq, proj.shape[1])
    kc3 = kc.reshape(batch, lc, kc.shape[1])
    projc3 = projc.reshape(batch, lc, projc.shape[1])
    blocks = (2 * _nbytes((tq, gw), BF16) + 2 * _nbytes((seq, HEAD_DIM), BF16) + 2 * _nbytes((lc, HEAD_DIM), BF16))
    temps = 6 * _nbytes((group * BLOCK, 3 * BLOCK + lc), F32)
    out = pl.pallas_call(
        functools.partial(_swa_kernel, tq=tq, seq=seq, group=group),
        grid=(batch, SWA_KV_HEADS, seq // tq),
        in_specs=[pl.BlockSpec(memory_space=pltpu.SMEM),
                  pl.BlockSpec((1, tq, gw), lambda b, h, i: (b, i, h)),
                  pl.BlockSpec((1, seq, HEAD_DIM), lambda b, h, i: (b, 0, n_heads + h)),
                  pl.BlockSpec((1, seq, HEAD_DIM), lambda b, h, i: (b, 0, v_col + h)),
                  pl.BlockSpec((1, lc, HEAD_DIM), lambda b, h, i: (b, 0, h)),
                  pl.BlockSpec((1, lc, HEAD_DIM), lambda b, h, i: (b, 0, v_col + h))],
        out_specs=pl.BlockSpec((1, tq, gw), lambda b, h, i: (b, i, h)),
        out_shape=jax.ShapeDtypeStruct((batch, seq, n_heads * HEAD_DIM), BF16),
        compiler_params=_params(("parallel", "parallel", "parallel"), blocks, temps),
        name="window_attention",
    )(sink, qk3, qk3, proj3, kc3, projc3)
    return out.reshape(batch * seq, n_heads * HEAD_DIM)


def _ctx_attn_kernel(*refs, group, lc, use_sink):
    if use_sink:
        sink_ref, q_ref, k_ref, v_ref, o_ref = refs
    else:
        q_ref, k_ref, v_ref, o_ref = refs
    q = jnp.concatenate([q_ref[0, :, g * HEAD_DIM:(g + 1) * HEAD_DIM] for g in range(group)], axis=0)
    s = _nt_dot(q, k_ref[0])
    m = jnp.max(s, -1, keepdims=True)
    if use_sink:
        sink = _sink_column(sink_ref, pl.program_id(1) * group, group, lc)
        m = jnp.maximum(m, sink)
    p = jnp.exp(s - m)
    den = jnp.sum(p, -1, keepdims=True)
    if use_sink:
        den = den + jnp.exp(sink - m)
    o = jnp.dot(p.astype(BF16), v_ref[0], preferred_element_type=F32) / den
    for g in range(group):
        o_ref[0, :, g * HEAD_DIM:(g + 1) * HEAD_DIM] = o[g * lc:(g + 1) * lc].astype(o_ref.dtype)


def _ctx_attn(qc, kc, projc, sink, *, n_heads, n_kv, v_col, batch):
    group = n_heads // n_kv
    lc = qc.shape[0] // batch
    gw = group * HEAD_DIM
    qc3 = qc.reshape(batch, lc, qc.shape[1])
    kc3 = kc.reshape(batch, lc, kc.shape[1])
    projc3 = projc.reshape(batch, lc, projc.shape[1])
    use_sink = sink is not None
    in_specs = [pl.BlockSpec((1, lc, gw), lambda b, h: (b, 0, h)),
                pl.BlockSpec((1, lc, HEAD_DIM), lambda b, h: (b, 0, h)),
                pl.BlockSpec((1, lc, HEAD_DIM), lambda b, h: (b, 0, v_col + h))]
    args = [qc3, kc3, projc3]
    if use_sink:
        in_specs = [pl.BlockSpec(memory_space=pltpu.SMEM)] + in_specs
        args = [sink] + args
    blocks = 2 * _nbytes((lc, gw), BF16) + 2 * _nbytes((lc, HEAD_DIM), BF16)
    out = pl.pallas_call(
        functools.partial(_ctx_attn_kernel, group=group, lc=lc, use_sink=use_sink),
        grid=(batch, n_kv),
        in_specs=in_specs,
        out_specs=pl.BlockSpec((1, lc, gw), lambda b, h: (b, 0, h)),
        out_shape=jax.ShapeDtypeStruct((batch, lc, n_heads * HEAD_DIM), BF16),
        compiler_params=_params(("parallel", "parallel"), blocks, 6 * _nbytes((group * lc, lc), F32)),
        name="context_attention",
    )(*args)
    return out.reshape(batch * lc, n_heads * HEAD_DIM)


def _nat_bias_kernel(rpb_ref, o_ref):
    h = pl.program_id(0)
    n_dc = 2 * NA_KW - 1
    w = lax.broadcasted_iota(jnp.int32, (GRID_W, 2 * GRID_W), 0)
    lane = lax.broadcasted_iota(jnp.int32, (GRID_W, 2 * GRID_W), 1)
    second = lane >= GRID_W
    kc = lane % GRID_W
    c0 = jnp.clip(w - NA_KW // 2, 0, GRID_W - NA_KW)
    dc = jnp.where((kc >= c0) & (kc < c0 + NA_KW), kc - w + (NA_KW - 1), -1)

    def pair(dr, carry):
        base = (h * (2 * NA_KH - 1) + dr) * n_dc
        tile = jnp.full((GRID_W, 2 * GRID_W), NEG_INF, F32)
        for j in range(n_dc):
            tile = jnp.where(dc == j, jnp.where(second, rpb_ref[base + n_dc + j], rpb_ref[base + j]), tile)
        o_ref[0, dr] = tile
        return carry

    lax.fori_loop(0, 2 * NA_KH - 2, pair, 0)


def _nat_bias(rpb):
    n_heads = rpb.shape[0]
    assert rpb.shape[1:] == (2 * NA_KH - 1, 2 * NA_KW - 1)
    shape = (n_heads, 2 * NA_KH - 2, GRID_W, 2 * GRID_W)
    return pl.pallas_call(
        _nat_bias_kernel,
        grid=(n_heads,),
        in_specs=[pl.BlockSpec(memory_space=pltpu.SMEM)],
        out_specs=pl.BlockSpec((1,) + shape[1:], lambda h: (h, 0, 0, 0)),
        out_shape=jax.ShapeDtypeStruct(shape, F32),
        compiler_params=_params(("parallel",), _nbytes(shape[1:], F32)),
        name="nat_bias_tiles",
    )(rpb.reshape(-1))


def _nat_kernel(q_ref, k_ref, v_ref, kc_ref, vc_ref, gq_ref, gk_ref, bias_ref, o_ref, kn_ref, *, rows, norm_rows,
                heads):
    n_pairs = NA_KH // 2
    gq = gq_ref[...] * ATTN_SCALE
    gk = gk_ref[...]
    n_ctx = kc_ref.shape[1] // HEAD_DIM

    for head in range(heads):
        cols = slice(head * HEAD_DIM, (head + 1) * HEAD_DIM)

        def norm_keys(i, carry, cols=cols):
            chunk = pl.ds(pl.multiple_of(i * norm_rows, norm_rows), norm_rows)
            kn_ref[chunk, :] = _rms_head(k_ref[0, chunk, cols], gk).astype(BF16)
            return carry

        lax.fori_loop(0, rows * GRID_W // norm_rows, norm_keys, 0)
        kc = _rms_head(kc_ref[0, :, cols], gk).astype(BF16)
        vc = vc_ref[0, :, cols]

        def scores(r, cols=cols, kc=kc):
            r0 = jnp.clip(r - NA_KH // 2, 0, rows - NA_KH)
            dr0 = r0 - r + (NA_KH - 1)
            q_rows = pl.ds(pl.multiple_of(r * GRID_W, GRID_W), GRID_W)
            k_rows = pl.ds(pl.multiple_of(r0 * GRID_W, GRID_W), NA_KH * GRID_W)
            q = _rms_head(q_ref[0, q_rows, cols], gq).astype(BF16)
            return q_rows, k_rows, dr0, _nt_dot(q, kn_ref[k_rows, :]), _nt_dot(q, kc)

        def attend(q_rows, k_rows, dr0, s_nb, s_c, cols=cols, vc=vc, head=head):
            s = [s_nb[:, 2 * GRID_W * j:2 * GRID_W * (j + 1)] + bias_ref[head, dr0 + 2 * j] for j in range(n_pairs)]
            s += [s_c[:, HEAD_DIM * j:HEAD_DIM * (j + 1)] for j in range(n_ctx)]
            m = jnp.max(functools.reduce(jnp.maximum, s), -1, keepdims=True)
            p = [jnp.exp(sj - m) for sj in s]
            den = jnp.sum(functools.reduce(jnp.add, p), -1, keepdims=True)
            p_nb = jnp.concatenate(p[:n_pairs], axis=-1).astype(BF16)
            p_c = jnp.concatenate(p[n_pairs:], axis=-1).astype(BF16)
            o = (jnp.dot(p_nb, v_ref[0, k_rows, cols], preferred_element_type=F32)
                 + jnp.dot(p_c, vc, preferred_element_type=F32)) / den
            o_ref[0, q_rows, cols] = o.astype(o_ref.dtype)

        def row_group(i, carry, scores=scores, attend=attend):
            staged = [scores(i * NAT_ROW_UNROLL + u) for u in range(NAT_ROW_UNROLL)]
            for row_state in staged:
                attend(*row_state)
            return carry

        lax.fori_loop(0, rows // NAT_ROW_UNROLL, row_group, 0)


def _nat(proj, projc, gq, gk, bias, *, n_heads, batch, seq):
    rows = seq // GRID_W
    assert seq % GRID_W == 0 and rows >= NA_KH and rows % NAT_ROW_UNROLL == 0
    lc = projc.shape[0] // batch
    assert lc % HEAD_DIM == 0
    heads = NAT_HEADS_PER_STEP if n_heads % NAT_HEADS_PER_STEP == 0 else 1
    width = heads * HEAD_DIM
    norm_rows = _tile(seq, 512, 16)
    proj3 = proj.reshape(batch, seq, proj.shape[1])
    projc3 = projc.reshape(batch, lc, projc.shape[1])

    def head_block(n, first_head):
        return pl.BlockSpec((1, n, width), lambda b, h: (b, 0, first_head // heads + h))

    gain_block = pl.BlockSpec((1, HEAD_DIM), lambda b, h: (0, 0))
    bias_block = (heads,) + bias.shape[1:]
    blocks = 4 * _nbytes((seq, width), BF16) + 2 * _nbytes((lc, width), BF16) + _nbytes(bias_block, F32)
    temps = (_nbytes((seq, HEAD_DIM), BF16)
             + 4 * NAT_ROW_UNROLL * _nbytes((GRID_W, NA_KH * GRID_W + lc), F32) + 4 * _nbytes((norm_rows, HEAD_DIM), F32))
    out = pl.pallas_call(
        functools.partial(_nat_kernel, rows=rows, norm_rows=norm_rows, heads=heads),
        grid=(batch, n_heads // heads),
        in_specs=[head_block(seq, 0), head_block(seq, n_heads), head_block(seq, 2 * n_heads),
                  head_block(lc, n_heads), head_block(lc, 2 * n_heads),
                  gain_block, gain_block,
                  pl.BlockSpec(bias_block, lambda b, h: (h, 0, 0, 0))],
        out_specs=head_block(seq, 0),
        out_shape=jax.ShapeDtypeStruct((batch, seq, n_heads * HEAD_DIM), BF16),
        scratch_shapes=[pltpu.VMEM((seq, HEAD_DIM), BF16)],
        compiler_params=_params(("parallel", "parallel"), blocks, temps),
        name="neighbourhood_attention",
    )(proj3, proj3, proj3, projc3, projc3, gq.reshape(1, HEAD_DIM), gk.reshape(1, HEAD_DIM), bias)
    return out.reshape(batch * seq, n_heads * HEAD_DIM)


def _gains(*groups):
    return jnp.concatenate([jnp.tile(g.astype(F32), n) for g, n in groups]).reshape(1, -1)


def kernel(x, c, ctx, c_ctx, w_ada, b_ada, norm_g, ffn_w_gate, ffn_w_up, ffn_w_down, mix_ab_w_in, mix_ab_w_out,
           gm_vnorm_g, gm_w_s, gm_b_s, swa_qnorm_g, swa_knorm_g, swa_sink, nat_w_qkv, nat_w_out, nat_qnorm_g,
           nat_knorm_g, nat_rpb):
    batch, seq, d = x.shape
    lc = ctx.shape[1]
    depth = w_ada.shape[0]
    ctx_row = batch
    assert batch + 1 <= MOD_ROWS

    cond = jnp.zeros((MOD_ROWS, d), F32).at[:batch].set(c).at[ctx_row].set(c_ctx)
    mod = _ada(cond, w_ada, b_ada).reshape(depth, MOD_ROWS, N_MOD, 1, d)

    lat_row = _ModRows(0, seq)
    ctx_row_of = _ModRows(ctx_row, batch * lc)

    def mod_vec(layer, sub, which):
        return mod[layer, :, 3 * sub + which]

    def ffn_in(xs, row, layer, sub):
        return _norm_mod(xs, norm_g[layer, sub], mod_vec(layer, sub, 0), mod_vec(layer, sub, 1), row)

    def ffn_out(a, wd, xs, row, layer, sub):
        return _residual_matmul([a], wd, xs, mod_vec(layer, sub, 2), row, HALF_STEP)

    def ffn_ctx(cs, layer, sub, wg, wu, wd):
        a, _ = _gate_up(ffn_in(cs, ctx_row_of, layer, sub), wg, wu)
        return ffn_out(a, wd, cs, ctx_row_of, layer, sub)

    def ffn_f32(layer, idx):
        return (_Weight(w, (layer, idx)) for w in (ffn_w_gate, ffn_w_up, ffn_w_down))

    xs = x.reshape(batch * seq, d)
    cs = ctx.reshape(batch * lc, d)
    rope = _rope_tables(seq)

    wg0_f32, wu0_f32, _ = ffn_f32(0, 0)
    wg, wu = (_Weight(w.array[w.lead].astype(BF16)) for w in (wg0_f32, wu0_f32))

    for layer in range(depth):
        last = layer == depth - 1
        j = layer // 2
        even = layer % 2 == 0
        mix_in_f32 = _Weight(mix_ab_w_in if even else nat_w_qkv, (j,))
        mix_out_f32 = _Weight(mix_ab_w_out if even else nat_w_out, (j,))
        _, _, wd_f32 = ffn_f32(layer, 0)
        a, (wd, mix_in, mix_out) = _gate_up(ffn_in(xs, lat_row, layer, 0), wg, wu, [wd_f32, mix_in_f32, mix_out_f32])
        xs = ffn_out(a, wd, xs, lat_row, layer, 0)
        cs = ffn_ctx(cs, layer, 0, wg, wu, wd)
        h = _norm_mod(xs, norm_g[layer, 1], mod_vec(layer, 1, 0), mod_vec(layer, 1, 1), lat_row)
        hc = _norm_mod(cs, norm_g[layer, 1], mod_vec(layer, 1, 0), mod_vec(layer, 1, 1), ctx_row_of)
        gate = mod_vec(layer, 1, 2)
        wg1_f32, wu1_f32, wd1_f32 = ffn_f32(layer, 1)
        proj, (wg, wu) = _matmul(h, mix_in, [wg1_f32, wu1_f32])
        projc, _ = _matmul(hc, mix_in)
        if even:
            n_groups = gm_w_s.shape[1]
            n_heads = swa_sink.shape[1]
            assert n_groups == n_heads
            q_col = 2 * n_groups
            v_col = q_col + n_heads + SWA_KV_HEADS
            qk_gain = _gains((swa_qnorm_g[j] * ATTN_SCALE, n_heads), (swa_knorm_g[j], SWA_KV_HEADS))
            qk = _head_norm(proj, q_col, qk_gain, rope)
            qkc = _head_norm(projc, q_col, qk_gain)
            kc = qkc[:, n_heads * HEAD_DIM:]
            y_g = _gmlp(proj, gm_w_s[j], gm_b_s[j], gm_vnorm_g[j])
            y_a = _swa(qk, proj, kc, projc, swa_sink[j], n_heads=n_heads, v_col=v_col, batch=batch, seq=seq)
            xs = _residual_matmul([y_g, y_a], mix_out, xs, gate, lat_row, 1.0)
            if not last:
                yc_g = _gmlp(projc, gm_w_s[j], gm_b_s[j], gm_vnorm_g[j])
                yc_a = _ctx_attn(qkc, kc, projc, swa_sink[j], n_heads=n_heads, n_kv=SWA_KV_HEADS, v_col=v_col,
                                 batch=batch)
                cs = _residual_matmul([yc_g, yc_a], mix_out, cs, gate, ctx_row_of, 1.0)
        else:
            n_heads = nat_rpb.shape[1]
            y = _nat(proj, projc, nat_qnorm_g[j], nat_knorm_g[j], _nat_bias(nat_rpb[j]), n_heads=n_heads,
                     batch=batch, seq=seq)
            xs = _residual_matmul([y], mix_out, xs, gate, lat_row, 1.0)
            if not last:
                qk_gain = _gains((nat_qnorm_g[j] * ATTN_SCALE, n_heads), (nat_knorm_g[j], n_heads))
                qkc = _head_norm(projc, 0, qk_gain)
                yc = _ctx_attn(qkc, qkc[:, n_heads * HEAD_DIM:], projc, None, n_heads=n_heads, n_kv=n_heads,
                               v_col=2 * n_heads, batch=batch)
                cs = _residual_matmul([yc], mix_out, cs, gate, ctx_row_of, 1.0)
        next_f32 = [] if last else list(ffn_f32(layer + 1, 0))[:2]
        a, (wd1, *next_w) = _gate_up(ffn_in(xs, lat_row, layer, 2), wg, wu, [wd1_f32] + next_f32)
        xs = ffn_out(a, wd1, xs, lat_row, layer, 2)
        if not last:
            cs = ffn_ctx(cs, layer, 2, wg, wu, wd1)
            wg, wu = next_w
    return xs.reshape(batch, seq, d)
```

```python
import functools
import math
from typing import NamedTuple

import jax
import jax.numpy as jnp
from jax import lax
from jax.experimental import pallas as pl
from jax.experimental.pallas import tpu as pltpu

F32 = jnp.float32
BF16 = jnp.bfloat16

HEAD_DIM = 128
GRID_W = 64
CHUNK = 128
WINDOW = 128
BLOCK = 128
SWA_KV_HEADS = 4
NA_KH = 8
NA_KW = 16
NAT_ROW_UNROLL = 8
NAT_HEADS_PER_STEP = 1
SWA_BLOCK_UNROLL = 2
HEAD_NORM_GROUP = 4
N_MOD = 9
HALF_STEP = 0.5
ROPE_THETA = 10000.0
NORM_EPS = 1e-6
NEG_INF = -1e30
ATTN_SCALE = HEAD_DIM ** -0.5

V7X_VMEM_BYTES = 64 * 1024 * 1024
VMEM_CEILING = V7X_VMEM_BYTES - 8 * 1024 * 1024
MOD_ROWS = 8
BF16_SUBLANES = 16
STREAM_SPLITS = 4
MXU_WIDTH = 256
NORM_ROW_GROUP = BF16_SUBLANES
NORM_GROUP_UNROLL = 4
LHS_BLOCK_BUDGET = 8 * 1024 * 1024


def _vmem_limit(pipelined_bytes, temp_bytes=0):
    need = int((2 * pipelined_bytes + temp_bytes) * 1.25) + (2 << 20)
    return min(max(need, 16 << 20), VMEM_CEILING)


def _fits_vmem(pipelined_bytes, temp_bytes=0):
    return 2 * pipelined_bytes + temp_bytes <= VMEM_CEILING * 9 // 10


def _params(semantics, pipelined_bytes, temp_bytes=0):
    return pltpu.CompilerParams(dimension_semantics=semantics,
                                vmem_limit_bytes=_vmem_limit(pipelined_bytes, temp_bytes))


def _tile(n, pref, quantum):
    if n <= pref:
        return n
    t = (pref // quantum) * quantum
    while t > quantum and n % t:
        t -= quantum
    assert n % t == 0, (n, pref, quantum)
    return t


def _nbytes(shape, dtype):
    return math.prod(shape) * jnp.dtype(dtype).itemsize


def _ada_kernel(c_ref, *refs):
    w_refs, (b_ref, o_ref) = refs[:-2], refs[-2:]
    c = c_ref[...]
    s = (c * jax.nn.sigmoid(c)).astype(BF16)
    k = w_refs[0].shape[1]
    acc = b_ref[0]
    for part, w_ref in enumerate(w_refs):
        acc = acc + jnp.dot(s[:, part * k:(part + 1) * k], w_ref[0].astype(BF16), preferred_element_type=F32)
    o_ref[0] = acc


def _ada(cond, w_ada, b_ada):
    depth, d, n = w_ada.shape
    tn = _tile(n, 512, 128)
    parts = STREAM_SPLITS if d % (STREAM_SPLITS * HEAD_DIM) == 0 else 1
    k = d // parts
    blocks = _nbytes((MOD_ROWS, d), F32) + _nbytes((d, tn), F32) + 2 * _nbytes((MOD_ROWS, tn), F32)
    w_specs = [pl.BlockSpec((1, k, tn), functools.partial(lambda l, j, p: (l, p, j), p=p)) for p in range(parts)]
    return pl.pallas_call(
        _ada_kernel,
        grid=(depth, n // tn),
        in_specs=[pl.BlockSpec((MOD_ROWS, d), lambda l, j: (0, 0))] + w_specs
        + [pl.BlockSpec((1, 1, tn), lambda l, j: (l, 0, j))],
        out_specs=pl.BlockSpec((1, MOD_ROWS, tn), lambda l, j: (l, 0, j)),
        out_shape=jax.ShapeDtypeStruct((depth, MOD_ROWS, n), F32),
        compiler_params=_params(("parallel", "parallel"), blocks, _nbytes((d, tn), BF16)),
        name="ada_modulation",
    )(cond, *([w_ada] * parts), b_ada.reshape(depth, 1, n))


def _norm_mod_kernel(*refs):
    x_refs, (g_ref, shift_ref, scale_ref, o_ref) = refs[:-4], refs[-4:]
    gain = g_ref[...] * (1.0 + scale_ref[0])
    shift = shift_ref[0]
    part_rows = x_refs[0].shape[0]

    for part, x_ref in enumerate(x_refs):
        def row_group(i, carry, x_ref=x_ref, part=part):
            start = pl.multiple_of(i * NORM_ROW_GROUP, NORM_ROW_GROUP)
            x = x_ref[pl.ds(start, NORM_ROW_GROUP), :]
            y = x * lax.rsqrt(jnp.mean(x * x, axis=-1, keepdims=True) + NORM_EPS)
            o_ref[pl.ds(part * part_rows + start, NORM_ROW_GROUP), :] = (y * gain + shift).astype(o_ref.dtype)
            return carry

        lax.fori_loop(0, part_rows // NORM_ROW_GROUP, row_group, 0, unroll=NORM_GROUP_UNROLL)


class _ModRows:
    def __init__(self, first, span):
        self.first, self.span = first, span

    def of_tile(self, i, tm):
        assert self.span % tm == 0
        return self.first + (i * tm) // self.span


def _norm_mod(x, g, shift, scale, mod_rows):
    t, d = x.shape
    tm = _tile(mod_rows.span, 256, NORM_ROW_GROUP)
    parts = STREAM_SPLITS if tm % (STREAM_SPLITS * NORM_ROW_GROUP * NORM_GROUP_UNROLL) == 0 else 1
    mod_spec = pl.BlockSpec((1, 1, d), lambda i: (mod_rows.of_tile(i, tm), 0, 0))
    blocks = _nbytes((tm, d), F32) + _nbytes((tm, d), BF16) + 3 * _nbytes((1, d), F32)
    x_specs = [pl.BlockSpec((tm // parts, d), functools.partial(lambda i, p: (i * parts + p, 0), p=p))
               for p in range(parts)]
    return pl.pallas_call(
        _norm_mod_kernel,
        grid=(t // tm,),
        in_specs=x_specs + [pl.BlockSpec((1, d), lambda i: (0, 0)), mod_spec, mod_spec],
        out_specs=pl.BlockSpec((tm, d), lambda i: (i, 0)),
        out_shape=jax.ShapeDtypeStruct((t, d), BF16),
        compiler_params=_params(("parallel",), blocks, 2 * _nbytes((tm, d), F32)),
        name="norm_modulate",
    )(*([x] * parts), g.reshape(1, d), shift, scale)


class _Weight(NamedTuple):
    array: jax.Array
    lead: tuple = ()

    @property
    def shape(self):
        return self.array.shape[-2:]


def _weight_spec(w, k, tn, k_block=0):
    return pl.BlockSpec((None,) * len(w.lead) + (k, tn), lambda i, j: w.lead + (k_block, j))


def _cast_rows(rows, n_steps):
    for r in range(BF16_SUBLANES, rows + 1, BF16_SUBLANES):
        if rows % r == 0 and rows // r <= n_steps:
            return r
    return None


class _SideCasts:
    def __init__(self, weights, grid):
        n_steps, n_inner = grid[0] * grid[1], grid[1]
        self.weights = list(weights)
        self.rows = [_cast_rows(w.shape[0], n_steps) for w in self.weights]
        self.carried = [w for w, r in zip(self.weights, self.rows) if r is not None]
        self.in_specs, self.out_specs, self.out_shapes, self.block_bytes = [], [], [], 0
        for w, r in zip(self.weights, self.rows):
            if r is None:
                continue
            rows, cols = w.shape
            block = functools.partial(lambda i, j, last: jnp.minimum(i * n_inner + j, last), last=rows // r - 1)
            self.in_specs.append(pl.BlockSpec((None,) * len(w.lead) + (r, cols),
                                              functools.partial(lambda i, j, w, block: w.lead + (block(i, j), 0),
                                                                w=w, block=block)))
            self.out_specs.append(pl.BlockSpec((r, cols), functools.partial(lambda i, j, block: (block(i, j), 0),
                                                                            block=block)))
            self.out_shapes.append(jax.ShapeDtypeStruct((rows, cols), BF16))
            self.block_bytes += _nbytes((r, cols), F32) + _nbytes((r, cols), BF16)

    @property
    def n(self):
        return len(self.carried)

    def inputs(self):
        return [w.array for w in self.carried]

    def results(self, outs):
        outs = iter(outs)
        return [_Weight(next(outs)) if r is not None else _Weight(w.array[w.lead].astype(BF16))
                for w, r in zip(self.weights, self.rows)]


def _column_halves(n):
    if n % (2 * MXU_WIDTH):
        return (slice(0, n),)
    return (slice(0, n // 2), slice(n // 2, n))


def _run_side_casts(in_refs, out_refs):
    for src, dst in zip(in_refs, out_refs):
        dst[...] = src[...].astype(dst.dtype)


def _matmul_kernel(a_ref, w_ref, *rest, n_side):
    o_ref = rest[n_side]
    _run_side_casts(rest[:n_side], rest[n_side + 1:])
    for cols in _column_halves(o_ref.shape[1]):
        o_ref[:, cols] = jnp.dot(a_ref[...], w_ref[:, cols], preferred_element_type=F32).astype(o_ref.dtype)


def _matmul(a, w, side_weights=()):
    t, k = a.shape
    n = w.shape[1]
    tm = _tile(t, 1024, 8)
    for tn in (_tile(n, 1024, 128), _tile(n, 512, 128)):
        grid = (t // tm, n // tn)
        side = _SideCasts(side_weights, grid)
        blocks = _nbytes((tm, k), BF16) + _nbytes((k, tn), BF16) + _nbytes((tm, tn), BF16) + side.block_bytes
        if _fits_vmem(blocks, _nbytes((tm, tn), F32)):
            break
    out, *casts = pl.pallas_call(
        functools.partial(_matmul_kernel, n_side=side.n),
        grid=grid,
        in_specs=[pl.BlockSpec((tm, k), lambda i, j: (i, 0)), _weight_spec(w, k, tn)] + side.in_specs,
        out_specs=[pl.BlockSpec((tm, tn), lambda i, j: (i, j))] + side.out_specs,
        out_shape=[jax.ShapeDtypeStruct((t, n), BF16)] + side.out_shapes,
        compiler_params=_params(("arbitrary", "arbitrary"), blocks, _nbytes((tm, tn), F32)),
        name="projection",
    )(a, w.array, *side.inputs())
    return out, side.results(casts)


def _gate_up_kernel(h_ref, wg_ref, wu_ref, *rest, n_side):
    o_ref = rest[n_side]
    _run_side_casts(rest[:n_side], rest[n_side + 1:])
    h = h_ref[...]
    for cols in _column_halves(o_ref.shape[1]):
        g = jnp.dot(h, wg_ref[:, cols], preferred_element_type=F32)
        u = jnp.dot(h, wu_ref[:, cols], preferred_element_type=F32)
        o_ref[:, cols] = (g * jax.nn.sigmoid(g) * u).astype(o_ref.dtype)


def _gate_up(h, wg, wu, side_weights=()):
    t, k = h.shape
    n = wg.shape[1]
    tm, tn = _tile(t, 1024, 8), _tile(n, 512, 128)
    grid = (t // tm, n // tn)
    side = _SideCasts(side_weights, grid)
    blocks = _nbytes((tm, k), BF16) + 2 * _nbytes((k, tn), BF16) + _nbytes((tm, tn), BF16) + side.block_bytes
    out, *casts = pl.pallas_call(
        functools.partial(_gate_up_kernel, n_side=side.n),
        grid=grid,
        in_specs=[pl.BlockSpec((tm, k), lambda i, j: (i, 0)), _weight_spec(wg, k, tn), _weight_spec(wu, k, tn)]
        + side.in_specs,
        out_specs=[pl.BlockSpec((tm, tn), lambda i, j: (i, j))] + side.out_specs,
        out_shape=[jax.ShapeDtypeStruct((t, n), BF16)] + side.out_shapes,
        compiler_params=_params(("arbitrary", "arbitrary"), blocks, 3 * _nbytes((tm, tn), F32)),
        name="ffn_gate_up",
    )(h, wg.array, wu.array, *side.inputs())
    return out, side.results(casts)


def _residual_kernel(*refs, n_parts, coef):
    a_refs, w_refs = refs[:n_parts], refs[n_parts:2 * n_parts]
    x_ref, gate_ref, o_ref = refs[2 * n_parts:]
    for cols in _column_halves(o_ref.shape[1]):
        acc = jnp.dot(a_refs[0][...], w_refs[0][:, cols], preferred_element_type=F32)
        for a_ref, w_ref in zip(a_refs[1:], w_refs[1:]):
            acc += jnp.dot(a_ref[...], w_ref[:, cols], preferred_element_type=F32)
        o_ref[:, cols] = x_ref[:, cols] + (coef * gate_ref[0, :, cols]) * acc


def _residual_matmul(parts, w, x, gate, mod_rows, coef):
    t, n = x.shape
    widths = [p.shape[1] for p in parts]
    assert len(set(widths)) == 1 and sum(widths) == w.shape[0]
    kp = widths[0]
    tm = _tile(mod_rows.span, 1024 if _nbytes((1024, sum(widths)), BF16) <= LHS_BLOCK_BUDGET else 512, 8)
    tn = _tile(n, 512, 128)
    blocks = (len(parts) * (_nbytes((tm, kp), BF16) + _nbytes((kp, tn), BF16))
              + 2 * _nbytes((tm, tn), F32) + _nbytes((1, tn), F32))
    in_specs = [pl.BlockSpec((tm, kp), lambda i, j: (i, 0)) for _ in parts]
    in_specs += [_weight_spec(w, kp, tn, k_block=p) for p in range(len(parts))]
    in_specs += [pl.BlockSpec((tm, tn), lambda i, j: (i, j)),
                 pl.BlockSpec((1, 1, tn), lambda i, j: (mod_rows.of_tile(i, tm), 0, j))]
    return pl.pallas_call(
        functools.partial(_residual_kernel, n_parts=len(parts), coef=coef),
        grid=(t // tm, n // tn),
        in_specs=in_specs,
        out_specs=pl.BlockSpec((tm, tn), lambda i, j: (i, j)),
        out_shape=jax.ShapeDtypeStruct((t, n), F32),
        compiler_params=_params(("parallel", "parallel"), blocks, _nbytes((tm, tn), F32)),
        name="residual_matmul",
    )(*parts, *([w.array] * len(parts)), x, gate)


def _gelu_tanh(x):
    return x * (0.5 * (1.0 + jnp.tanh(math.sqrt(2.0 / math.pi) * (x + 0.044715 * (x * x * x)))))


def _gmlp_kernel(uv_ref, ws_ref, bs_ref, vg_ref, o_ref, *, n_groups, n_chunks):
    for g in range(n_groups):
        u_cols = slice(g * HEAD_DIM, (g + 1) * HEAD_DIM)
        v_cols = slice((n_groups + g) * HEAD_DIM, (n_groups + g + 1) * HEAD_DIM)
        w = ws_ref[g]
        b = bs_ref[g]
        vg = vg_ref[:, u_cols]

        for c in range(n_chunks):
            rows = slice(c * CHUNK, (c + 1) * CHUNK)
            u = _gelu_tanh(uv_ref[rows, u_cols].astype(F32))
            v = _gelu_tanh(uv_ref[rows, v_cols].astype(F32))
            v = v - jnp.mean(v, axis=-1, keepdims=True)
            vn = (v * lax.rsqrt(jnp.mean(v * v, axis=-1, keepdims=True) + NORM_EPS)) * vg
            s = jnp.dot(w, vn.astype(BF16), preferred_element_type=F32) + b
            o_ref[rows, u_cols] = (u * s).astype(o_ref.dtype)


def _gmlp(proj, w_s, b_s, vnorm_g):
    t = proj.shape[0]
    n_groups = w_s.shape[0]
    gm = n_groups * HEAD_DIM
    tc = _tile(t, 512, CHUNK)
    blocks = (_nbytes((tc, 2 * gm), BF16) + _nbytes((tc, gm), BF16) + _nbytes(w_s.shape, BF16)
              + _nbytes((n_groups, CHUNK, 128), F32) + _nbytes((8, gm), F32))
    return pl.pallas_call(
        functools.partial(_gmlp_kernel, n_groups=n_groups, n_chunks=tc // CHUNK),
        grid=(t // tc,),
        in_specs=[pl.BlockSpec((tc, 2 * gm), lambda i: (i, 0)),
                  pl.BlockSpec(w_s.shape, lambda i: (0, 0, 0)),
                  pl.BlockSpec((n_groups, CHUNK, 1), lambda i: (0, 0, 0)),
                  pl.BlockSpec((1, gm), lambda i: (0, 0))],
        out_specs=pl.BlockSpec((tc, gm), lambda i: (i, 0)),
        out_shape=jax.ShapeDtypeStruct((t, gm), BF16),
        compiler_params=_params(("parallel",), blocks, 16 * _nbytes((CHUNK, HEAD_DIM), F32)),
        name="gmlp_chunk_gating",
    )(proj, w_s.astype(BF16), b_s.reshape(n_groups, CHUNK, 1), vnorm_g.reshape(1, gm))


def _swap_rope_halves(x):
    lane = lax.broadcasted_iota(jnp.int32, x.shape, x.ndim - 1)
    return jnp.where(lane % 64 < 32, pltpu.roll(x, HEAD_DIM - 32, x.ndim - 1), pltpu.roll(x, 32, x.ndim - 1))


def _rms_head(x, g):
    x = x.astype(F32)
    return (x * lax.rsqrt(jnp.mean(x * x, axis=-1, keepdims=True) + NORM_EPS)) * g


def _head_norm_kernel(x_ref, g_ref, *rest, rope, heads):
    o_ref = rest[-1]
    for h in range(heads):
        cols = slice(h * HEAD_DIM, (h + 1) * HEAD_DIM)
        y = _rms_head(x_ref[:, cols], g_ref[:, cols])
        if rope:
            cos_ref, sin_ref = rest[:2]
            y = y * cos_ref[...] + _swap_rope_halves(y) * sin_ref[...]
        o_ref[:, cols] = y.astype(o_ref.dtype)


def _head_norm(src, first_head, gains, rope_tables=None):
    t = src.shape[0]
    n_heads = gains.shape[1] // HEAD_DIM
    heads = HEAD_NORM_GROUP if first_head % HEAD_NORM_GROUP == 0 and n_heads % HEAD_NORM_GROUP == 0 else 1
    width = heads * HEAD_DIM
    seq = rope_tables[0].shape[0] if rope_tables is not None else t
    tm = _tile(seq, 1024, 16)
    per_seq = seq // tm
    in_specs = [pl.BlockSpec((tm, width), lambda i, h: (i, first_head // heads + h)),
                pl.BlockSpec((1, width), lambda i, h: (0, h))]
    args = [src, gains]
    if rope_tables is not None:
        in_specs += [pl.BlockSpec((tm, HEAD_DIM), lambda i, h: (i % per_seq, 0))] * 2
        args += list(rope_tables)
    blocks = 2 * _nbytes((tm, width), BF16) + 2 * _nbytes((tm, HEAD_DIM), F32)
    return pl.pallas_call(
        functools.partial(_head_norm_kernel, rope=rope_tables is not None, heads=heads),
        grid=(t // tm, n_heads // heads),
        in_specs=in_specs,
        out_specs=pl.BlockSpec((tm, width), lambda i, h: (i, h)),
        out_shape=jax.ShapeDtypeStruct((t, n_heads * HEAD_DIM), BF16),
        compiler_params=_params(("parallel", "parallel"), blocks, 6 * heads * _nbytes((tm, HEAD_DIM), F32)),
        name="head_norm_rope" if rope_tables is not None else "head_norm",
    )(*args)


def _rope_tables(seq):
    half = HEAD_DIM // 2
    t = jnp.arange(seq)
    rows = (t // GRID_W).astype(F32)
    cols = (t % GRID_W).astype(F32)
    inv = 1.0 / (ROPE_THETA ** (jnp.arange(0, half, 2, dtype=F32) / half))
    ang = jnp.concatenate([rows[:, None] * inv, rows[:, None] * inv, cols[:, None] * inv, cols[:, None] * inv], -1)
    sign = jnp.where(jnp.arange(HEAD_DIM) % half < half // 2, -1.0, 1.0).astype(F32)
    return jnp.cos(ang), jnp.sin(ang) * sign


def _nt_dot(a, b):
    return lax.dot_general(a, b, (((1,), (1,)), ((), ())), preferred_element_type=F32)


def _sink_column(sink_ref, first, group, rows_per_head):
    head = lax.broadcasted_iota(jnp.int32, (group * rows_per_head, 1), 0) // rows_per_head
    col = jnp.zeros((group * rows_per_head, 1), F32)
    for g in range(group):
        col = jnp.where(head == g, sink_ref[first + g], col)
    return col


def _swa_kernel(sink_ref, q_ref, k_ref, v_ref, kc_ref, vc_ref, o_ref, *, tq, seq, group):
    kv_head, q_tile = pl.program_id(1), pl.program_id(2)
    kc, vc = kc_ref[0], vc_ref[0]
    sink = _sink_column(sink_ref, kv_head * group, group, BLOCK)
    q_off = lax.broadcasted_iota(jnp.int32, (group * BLOCK, 1), 0) % BLOCK
    k_off = lax.broadcasted_iota(jnp.int32, (1, 3 * BLOCK), 1)

    def scores(j):
        base = q_tile * tq + j * BLOCK
        start = pl.multiple_of(jnp.clip(base - BLOCK, 0, seq - 3 * BLOCK), BLOCK)
        q_rows = pl.ds(pl.multiple_of(j * BLOCK, BLOCK), BLOCK)
        q = jnp.concatenate([q_ref[0, q_rows, g * HEAD_DIM:(g + 1) * HEAD_DIM] for g in range(group)], axis=0)
        s_w = _nt_dot(q, k_ref[0, pl.ds(start, 3 * BLOCK), :])
        return base, start, q_rows, s_w, _nt_dot(q, kc)

    def attend(base, start, q_rows, s_w, s_c):
        s_w = jnp.where(jnp.abs((start + k_off) - (base + q_off)) <= WINDOW, s_w, NEG_INF)
        m = jnp.maximum(jnp.maximum(jnp.max(s_w, -1, keepdims=True), jnp.max(s_c, -1, keepdims=True)), sink)
        p_w = jnp.exp(s_w - m)
        p_c = jnp.exp(s_c - m)
        den = jnp.sum(p_w, -1, keepdims=True) + jnp.sum(p_c, -1, keepdims=True) + jnp.exp(sink - m)
        o = (jnp.dot(p_w.astype(BF16), v_ref[0, pl.ds(start, 3 * BLOCK), :], preferred_element_type=F32)
             + jnp.dot(p_c.astype(BF16), vc, preferred_element_type=F32)) / den
        for g in range(group):
            o_ref[0, q_rows, g * HEAD_DIM:(g + 1) * HEAD_DIM] = o[g * BLOCK:(g + 1) * BLOCK].astype(o_ref.dtype)

    def block_group(i, carry):
        staged = [scores(i * SWA_BLOCK_UNROLL + u) for u in range(SWA_BLOCK_UNROLL)]
        for block_state in staged:
            attend(*block_state)
        return carry

    assert (tq // BLOCK) % SWA_BLOCK_UNROLL == 0
    lax.fori_loop(0, tq // BLOCK // SWA_BLOCK_UNROLL, block_group, 0)


def _swa(qk, proj, kc, projc, sink, *, n_heads, v_col, batch, seq):
    group = n_heads // SWA_KV_HEADS
    lc = kc.shape[0] // batch
    gw = group * HEAD_DIM
    tq = _tile(seq, 1024, BLOCK)
    qk3 = qk.reshape(batch, seq, qk.shape[1])
    proj3 = proj.reshape(batch, seq, proj.shape[1])
    kc3 = kc.reshape(batch, lc, kc.shape[1])
    projc3 = projc.reshape(batch, lc, projc.shape[1])
    blocks = (2 * _nbytes((tq, gw), BF16) + 2 * _nbytes((seq, HEAD_DIM), BF16) + 2 * _nbytes((lc, HEAD_DIM), BF16))
    temps = 6 * _nbytes((group * BLOCK, 3 * BLOCK + lc), F32)
    out = pl.pallas_call(
        functools.partial(_swa_kernel, tq=tq, seq=seq, group=group),
        grid=(batch, SWA_KV_HEADS, seq // tq),
        in_specs=[pl.BlockSpec(memory_space=pltpu.SMEM),
                  pl.BlockSpec((1, tq, gw), lambda b, h, i: (b, i, h)),
                  pl.BlockSpec((1, seq, HEAD_DIM), lambda b, h, i: (b, 0, n_heads + h)),
                  pl.BlockSpec((1, seq, HEAD_DIM), lambda b, h, i: (b, 0, v_col + h)),
                  pl.BlockSpec((1, lc, HEAD_DIM), lambda b, h, i: (b, 0, h)),
                  pl.BlockSpec((1, lc, HEAD_DIM), lambda b, h, i: (b, 0, v_col + h))],
        out_specs=pl.BlockSpec((1, tq, gw), lambda b, h, i: (b, i, h)),
        out_shape=jax.ShapeDtypeStruct((batch, seq, n_heads * HEAD_DIM), BF16),
        compiler_params=_params(("parallel", "parallel", "parallel"), blocks, temps),
        name="window_attention",
    )(sink, qk3, qk3, proj3, kc3, projc3)
    return out.reshape(batch * seq, n_heads * HEAD_DIM)


def _ctx_attn_kernel(*refs, group, lc, use_sink):
    if use_sink:
        sink_ref, q_ref, k_ref, v_ref, o_ref = refs
    else:
        q_ref, k_ref, v_ref, o_ref = refs
    q = jnp.concatenate([q_ref[0, :, g * HEAD_DIM:(g + 1) * HEAD_DIM] for g in range(group)], axis=0)
    s = _nt_dot(q, k_ref[0])
    m = jnp.max(s, -1, keepdims=True)
    if use_sink:
        sink = _sink_column(sink_ref, pl.program_id(1) * group, group, lc)
        m = jnp.maximum(m, sink)
    p = jnp.exp(s - m)
    den = jnp.sum(p, -1, keepdims=True)
    if use_sink:
        den = den + jnp.exp(sink - m)
    o = jnp.dot(p.astype(BF16), v_ref[0], preferred_element_type=F32) / den
    for g in range(group):
        o_ref[0, :, g * HEAD_DIM:(g + 1) * HEAD_DIM] = o[g * lc:(g + 1) * lc].astype(o_ref.dtype)


def _ctx_attn(qc, kc, projc, sink, *, n_heads, n_kv, v_col, batch):
    group = n_heads // n_kv
    lc = qc.shape[0] // batch
    gw = group * HEAD_DIM
    qc3 = qc.reshape(batch, lc, qc.shape[1])
    kc3 = kc.reshape(batch, lc, kc.shape[1])
    projc3 = projc.reshape(batch, lc, projc.shape[1])
    use_sink = sink is not None
    in_specs = [pl.BlockSpec((1, lc, gw), lambda b, h: (b, 0, h)),
                pl.BlockSpec((1, lc, HEAD_DIM), lambda b, h: (b, 0, h)),
                pl.BlockSpec((1, lc, HEAD_DIM), lambda b, h: (b, 0, v_col + h))]
    args = [qc3, kc3, projc3]
    if use_sink:
        in_specs = [pl.BlockSpec(memory_space=pltpu.SMEM)] + in_specs
        args = [sink] + args
    blocks = 2 * _nbytes((lc, gw), BF16) + 2 * _nbytes((lc, HEAD_DIM), BF16)
    out = pl.pallas_call(
        functools.partial(_ctx_attn_kernel, group=group, lc=lc, use_sink=use_sink),
        grid=(batch, n_kv),
        in_specs=in_specs,
        out_specs=pl.BlockSpec((1, lc, gw), lambda b, h: (b, 0, h)),
        out_shape=jax.ShapeDtypeStruct((batch, lc, n_heads * HEAD_DIM), BF16),
        compiler_params=_params(("parallel", "parallel"), blocks, 6 * _nbytes((group * lc, lc), F32)),
        name="context_attention",
    )(*args)
    return out.reshape(batch * lc, n_heads * HEAD_DIM)


def _nat_bias_kernel(rpb_ref, o_ref):
    h = pl.program_id(0)
    n_dc = 2 * NA_KW - 1
    w = lax.broadcasted_iota(jnp.int32, (GRID_W, 2 * GRID_W), 0)
    lane = lax.broadcasted_iota(jnp.int32, (GRID_W, 2 * GRID_W), 1)
    second = lane >= GRID_W
    kc = lane % GRID_W
    c0 = jnp.clip(w - NA_KW // 2, 0, GRID_W - NA_KW)
    dc = jnp.where((kc >= c0) & (kc < c0 + NA_KW), kc - w + (NA_KW - 1), -1)

    def pair(dr, carry):
        base = (h * (2 * NA_KH - 1) + dr) * n_dc
        tile = jnp.full((GRID_W, 2 * GRID_W), NEG_INF, F32)
        for j in range(n_dc):
            tile = jnp.where(dc == j, jnp.where(second, rpb_ref[base + n_dc + j], rpb_ref[base + j]), tile)
        o_ref[0, dr] = tile
        return carry

    lax.fori_loop(0, 2 * NA_KH - 2, pair, 0)


def _nat_bias(rpb):
    n_heads = rpb.shape[0]
    assert rpb.shape[1:] == (2 * NA_KH - 1, 2 * NA_KW - 1)
    shape = (n_heads, 2 * NA_KH - 2, GRID_W, 2 * GRID_W)
    return pl.pallas_call(
        _nat_bias_kernel,
        grid=(n_heads,),
        in_specs=[pl.BlockSpec(memory_space=pltpu.SMEM)],
        out_specs=pl.BlockSpec((1,) + shape[1:], lambda h: (h, 0, 0, 0)),
        out_shape=jax.ShapeDtypeStruct(shape, F32),
        compiler_params=_params(("parallel",), _nbytes(shape[1:], F32)),
        name="nat_bias_tiles",
    )(rpb.reshape(-1))


def _nat_kernel(q_ref, k_ref, v_ref, kc_ref, vc_ref, gq_ref, gk_ref, bias_ref, o_ref, kn_ref, *, rows, norm_rows,
                heads):
    n_pairs = NA_KH // 2
    gq = gq_ref[...] * ATTN_SCALE
    gk = gk_ref[...]
    n_ctx = kc_ref.shape[1] // HEAD_DIM

    for head in range(heads):
        cols = slice(head * HEAD_DIM, (head + 1) * HEAD_DIM)

        def norm_keys(i, carry, cols=cols):
            chunk = pl.ds(pl.multiple_of(i * norm_rows, norm_rows), norm_rows)
            kn_ref[chunk, :] = _rms_head(k_ref[0, chunk, cols], gk).astype(BF16)
            return carry

        lax.fori_loop(0, rows * GRID_W // norm_rows, norm_keys, 0)
        kc = _rms_head(kc_ref[0, :, cols], gk).astype(BF16)
        vc = vc_ref[0, :, cols]

        def window(r):
            r0 = jnp.clip(r - NA_KH // 2, 0, rows - NA_KH)
            q_rows = pl.ds(pl.multiple_of(r * GRID_W, GRID_W), GRID_W)
            k_rows = pl.ds(pl.multiple_of(r0 * GRID_W, GRID_W), NA_KH * GRID_W)
            return q_rows, k_rows, r0 - r + (NA_KH - 1)

        def softmax_row(dr0, s_nb, s_c, head=head):
            s = [s_nb[:, 2 * GRID_W * j:2 * GRID_W * (j + 1)] + bias_ref[head, dr0 + 2 * j] for j in range(n_pairs)]
            s += [s_c[:, HEAD_DIM * j:HEAD_DIM * (j + 1)] for j in range(n_ctx)]
            m = jnp.max(functools.reduce(jnp.maximum, s), -1, keepdims=True)
            p = [jnp.exp(sj - m) for sj in s]
            den = jnp.sum(functools.reduce(jnp.add, p), -1, keepdims=True)
            return (jnp.concatenate(p[:n_pairs], axis=-1).astype(BF16),
                    jnp.concatenate(p[n_pairs:], axis=-1).astype(BF16), den)

        def row_group(i, carry, cols=cols, kc=kc, vc=vc, window=window, softmax_row=softmax_row):
            wins = [window(i * NAT_ROW_UNROLL + u) for u in range(NAT_ROW_UNROLL)]
            q = [_rms_head(q_ref[0, q_rows, cols], gq).astype(BF16) for q_rows, _, _ in wins]
            s_c = _nt_dot(jnp.concatenate(q, axis=0), kc)
            s_nb = [_nt_dot(qu, kn_ref[k_rows, :]) for qu, (_, k_rows, _) in zip(q, wins)]
            o_nb, p_c, den = [], [], []
            for u, (_, k_rows, dr0) in enumerate(wins):
                p_nb_u, p_c_u, den_u = softmax_row(dr0, s_nb[u], s_c[u * GRID_W:(u + 1) * GRID_W])
                o_nb.append(jnp.dot(p_nb_u, v_ref[0, k_rows, cols], preferred_element_type=F32))
                p_c.append(p_c_u)
                den.append(den_u)
            o_c = jnp.dot(jnp.concatenate(p_c, axis=0), vc, preferred_element_type=F32)
            for u, (q_rows, _, _) in enumerate(wins):
                o = (o_nb[u] + o_c[u * GRID_W:(u + 1) * GRID_W]) / den[u]
                o_ref[0, q_rows, cols] = o.astype(o_ref.dtype)
            return carry

        lax.fori_loop(0, rows // NAT_ROW_UNROLL, row_group, 0)


def _nat(proj, projc, gq, gk, bias, *, n_heads, batch, seq):
    rows = seq // GRID_W
    assert seq % GRID_W == 0 and rows >= NA_KH and rows % NAT_ROW_UNROLL == 0
    lc = projc.shape[0] // batch
    assert lc % HEAD_DIM == 0
    heads = NAT_HEADS_PER_STEP if n_heads % NAT_HEADS_PER_STEP == 0 else 1
    width = heads * HEAD_DIM
    norm_rows = _tile(seq, 512, 16)
    proj3 = proj.reshape(batch, seq, proj.shape[1])
    projc3 = projc.reshape(batch, lc, projc.shape[1])

    def head_block(n, first_head):
        return pl.BlockSpec((1, n, width), lambda b, h: (b, 0, first_head // heads + h))

    gain_block = pl.BlockSpec((1, HEAD_DIM), lambda b, h: (0, 0))
    bias_block = (heads,) + bias.shape[1:]
    blocks = 4 * _nbytes((seq, width), BF16) + 2 * _nbytes((lc, width), BF16) + _nbytes(bias_block, F32)
    temps = (_nbytes((seq, HEAD_DIM), BF16)
             + 4 * NAT_ROW_UNROLL * _nbytes((GRID_W, NA_KH * GRID_W + lc), F32) + 4 * _nbytes((norm_rows, HEAD_DIM), F32))
    out = pl.pallas_call(
        functools.partial(_nat_kernel, rows=rows, norm_rows=norm_rows, heads=heads),
        grid=(batch, n_heads // heads),
        in_specs=[head_block(seq, 0), head_block(seq, n_heads), head_block(seq, 2 * n_heads),
                  head_block(lc, n_heads), head_block(lc, 2 * n_heads),
                  gain_block, gain_block,
                  pl.BlockSpec(bias_block, lambda b, h: (h, 0, 0, 0))],
        out_specs=head_block(seq, 0),
        out_shape=jax.ShapeDtypeStruct((batch, seq, n_heads * HEAD_DIM), BF16),
        scratch_shapes=[pltpu.VMEM((seq, HEAD_DIM), BF16)],
        compiler_params=_params(("parallel", "parallel"), blocks, temps),
        name="neighbourhood_attention",
    )(proj3, proj3, proj3, projc3, projc3, gq.reshape(1, HEAD_DIM), gk.reshape(1, HEAD_DIM), bias)
    return out.reshape(batch * seq, n_heads * HEAD_DIM)


def _gains(*groups):
    return jnp.concatenate([jnp.tile(g.astype(F32), n) for g, n in groups]).reshape(1, -1)


def kernel(x, c, ctx, c_ctx, w_ada, b_ada, norm_g, ffn_w_gate, ffn_w_up, ffn_w_down, mix_ab_w_in, mix_ab_w_out,
           gm_vnorm_g, gm_w_s, gm_b_s, swa_qnorm_g, swa_knorm_g, swa_sink, nat_w_qkv, nat_w_out, nat_qnorm_g,
           nat_knorm_g, nat_rpb):
    batch, seq, d = x.shape
    lc = ctx.shape[1]
    depth = w_ada.shape[0]
    ctx_row = batch
    assert batch + 1 <= MOD_ROWS

    cond = jnp.zeros((MOD_ROWS, d), F32).at[:batch].set(c).at[ctx_row].set(c_ctx)
    mod = _ada(cond, w_ada, b_ada).reshape(depth, MOD_ROWS, N_MOD, 1, d)

    lat_row = _ModRows(0, seq)
    ctx_row_of = _ModRows(ctx_row, batch * lc)

    def mod_vec(layer, sub, which):
        return mod[layer, :, 3 * sub + which]

    def ffn_in(xs, row, layer, sub):
        return _norm_mod(xs, norm_g[layer, sub], mod_vec(layer, sub, 0), mod_vec(layer, sub, 1), row)

    def ffn_out(a, wd, xs, row, layer, sub):
        return _residual_matmul([a], wd, xs, mod_vec(layer, sub, 2), row, HALF_STEP)

    def ffn_ctx(cs, layer, sub, wg, wu, wd):
        a, _ = _gate_up(ffn_in(cs, ctx_row_of, layer, sub), wg, wu)
        return ffn_out(a, wd, cs, ctx_row_of, layer, sub)

    def ffn_f32(layer, idx):
        return (_Weight(w, (layer, idx)) for w in (ffn_w_gate, ffn_w_up, ffn_w_down))

    xs = x.reshape(batch * seq, d)
    cs = ctx.reshape(batch * lc, d)
    rope = _rope_tables(seq)

    wg0_f32, wu0_f32, _ = ffn_f32(0, 0)
    wg, wu = (_Weight(w.array[w.lead].astype(BF16)) for w in (wg0_f32, wu0_f32))

    for layer in range(depth):
        last = layer == depth - 1
        j = layer // 2
        even = layer % 2 == 0
        mix_in_f32 = _Weight(mix_ab_w_in if even else nat_w_qkv, (j,))
        mix_out_f32 = _Weight(mix_ab_w_out if even else nat_w_out, (j,))
        _, _, wd_f32 = ffn_f32(layer, 0)
        a, (wd, mix_in, mix_out) = _gate_up(ffn_in(xs, lat_row, layer, 0), wg, wu, [wd_f32, mix_in_f32, mix_out_f32])
        xs = ffn_out(a, wd, xs, lat_row, layer, 0)
        cs = ffn_ctx(cs, layer, 0, wg, wu, wd)
        h = _norm_mod(xs, norm_g[layer, 1], mod_vec(layer, 1, 0), mod_vec(layer, 1, 1), lat_row)
        hc = _norm_mod(cs, norm_g[layer, 1], mod_vec(layer, 1, 0), mod_vec(layer, 1, 1), ctx_row_of)
        gate = mod_vec(layer, 1, 2)
        wg1_f32, wu1_f32, wd1_f32 = ffn_f32(layer, 1)
        proj, (wg, wu) = _matmul(h, mix_in, [wg1_f32, wu1_f32])
        projc, _ = _matmul(hc, mix_in)
        if even:
            n_groups = gm_w_s.shape[1]
            n_heads = swa_sink.shape[1]
            assert n_groups == n_heads
            q_col = 2 * n_groups
            v_col = q_col + n_heads + SWA_KV_HEADS
            qk_gain = _gains((swa_qnorm_g[j] * ATTN_SCALE, n_heads), (swa_knorm_g[j], SWA_KV_HEADS))
            qk = _head_norm(proj, q_col, qk_gain, rope)
            qkc = _head_norm(projc, q_col, qk_gain)
            kc = qkc[:, n_heads * HEAD_DIM:]
            y_g = _gmlp(proj, gm_w_s[j], gm_b_s[j], gm_vnorm_g[j])
            y_a = _swa(qk, proj, kc, projc, swa_sink[j], n_heads=n_heads, v_col=v_col, batch=batch, seq=seq)
            xs = _residual_matmul([y_g, y_a], mix_out, xs, gate, lat_row, 1.0)
            if not last:
                yc_g = _gmlp(projc, gm_w_s[j], gm_b_s[j], gm_vnorm_g[j])
                yc_a = _ctx_attn(qkc, kc, projc, swa_sink[j], n_heads=n_heads, n_kv=SWA_KV_HEADS, v_col=v_col,
                                 batch=batch)
                cs = _residual_matmul([yc_g, yc_a], mix_out, cs, gate, ctx_row_of, 1.0)
        else:
            n_heads = nat_rpb.shape[1]
            y = _nat(proj, projc, nat_qnorm_g[j], nat_knorm_g[j], _nat_bias(nat_rpb[j]), n_heads=n_heads,
                     batch=batch, seq=seq)
            xs = _residual_matmul([y], mix_out, xs, gate, lat_row, 1.0)
            if not last:
                qk_gain = _gains((nat_qnorm_g[j] * ATTN_SCALE, n_heads), (nat_knorm_g[j], n_heads))
                qkc = _head_norm(projc, 0, qk_gain)
                yc = _ctx_attn(qkc, qkc[:, n_heads * HEAD_DIM:], projc, None, n_heads=n_heads, n_kv=n_heads,
                               v_col=2 * n_heads, batch=batch)
                cs = _residual_matmul([yc], mix_out, cs, gate, ctx_row_of, 1.0)
        next_f32 = [] if last else list(ffn_f32(layer + 1, 0))[:2]
        a, (wd1, *next_w) = _gate_up(ffn_in(xs, lat_row, layer, 2), wg, wu, [wd1_f32] + next_f32)
        xs = ffn_out(a, wd1, xs, lat_row, layer, 2)
        if not last:
            cs = ffn_ctx(cs, layer, 2, wg, wu, wd1)
            wg, wu = next_w
    return xs.reshape(batch, seq, d)
```

```python
import functools
import math
from typing import NamedTuple

import jax
import jax.numpy as jnp
from jax import lax
from jax.experimental import pallas as pl
from jax.experimental.pallas import tpu as pltpu

F32 = jnp.float32
BF16 = jnp.bfloat16

HEAD_DIM = 128
GRID_W = 64
CHUNK = 128
WINDOW = 128
BLOCK = 128
SWA_KV_HEADS = 4
NA_KH = 8
NA_KW = 16
NAT_ROW_UNROLL = 16
NAT_HEADS_PER_STEP = 1
SWA_BLOCK_UNROLL = 4
HEAD_NORM_GROUP = 4
N_MOD = 9
HALF_STEP = 0.5
ROPE_THETA = 10000.0
NORM_EPS = 1e-6
NEG_INF = -1e30
ATTN_SCALE = HEAD_DIM ** -0.5

V7X_VMEM_BYTES = 64 * 1024 * 1024
VMEM_CEILING = V7X_VMEM_BYTES - 8 * 1024 * 1024
MOD_ROWS = 8
BF16_SUBLANES = 16
STREAM_SPLITS = 4
MXU_WIDTH = 256
NORM_ROW_GROUP = BF16_SUBLANES
NORM_GROUP_UNROLL = 4
LHS_BLOCK_BUDGET = 8 * 1024 * 1024


def _vmem_limit(pipelined_bytes, temp_bytes=0):
    need = int((2 * pipelined_bytes + temp_bytes) * 1.25) + (2 << 20)
    return min(max(need, 16 << 20), VMEM_CEILING)


def _fits_vmem(pipelined_bytes, temp_bytes=0):
    return 2 * pipelined_bytes + temp_bytes <= VMEM_CEILING * 9 // 10


def _params(semantics, pipelined_bytes, temp_bytes=0):
    return pltpu.CompilerParams(dimension_semantics=semantics,
                                vmem_limit_bytes=_vmem_limit(pipelined_bytes, temp_bytes))


def _tile(n, pref, quantum):
    if n <= pref:
        return n
    t = (pref // quantum) * quantum
    while t > quantum and n % t:
        t -= quantum
    assert n % t == 0, (n, pref, quantum)
    return t


def _nbytes(shape, dtype):
    return math.prod(shape) * jnp.dtype(dtype).itemsize


def _ada_kernel(c_ref, *refs):
    w_refs, (b_ref, o_ref) = refs[:-2], refs[-2:]
    c = c_ref[...]
    s = (c * jax.nn.sigmoid(c)).astype(BF16)
    k = w_refs[0].shape[1]
    acc = b_ref[0]
    for part, w_ref in enumerate(w_refs):
        acc = acc + jnp.dot(s[:, part * k:(part + 1) * k], w_ref[0].astype(BF16), preferred_element_type=F32)
    o_ref[0] = acc


def _ada(cond, w_ada, b_ada):
    depth, d, n = w_ada.shape
    tn = _tile(n, 512, 128)
    parts = STREAM_SPLITS if d % (STREAM_SPLITS * HEAD_DIM) == 0 else 1
    k = d // parts
    blocks = _nbytes((MOD_ROWS, d), F32) + _nbytes((d, tn), F32) + 2 * _nbytes((MOD_ROWS, tn), F32)
    w_specs = [pl.BlockSpec((1, k, tn), functools.partial(lambda l, j, p: (l, p, j), p=p)) for p in range(parts)]
    return pl.pallas_call(
        _ada_kernel,
        grid=(depth, n // tn),
        in_specs=[pl.BlockSpec((MOD_ROWS, d), lambda l, j: (0, 0))] + w_specs
        + [pl.BlockSpec((1, 1, tn), lambda l, j: (l, 0, j))],
        out_specs=pl.BlockSpec((1, MOD_ROWS, tn), lambda l, j: (l, 0, j)),
        out_shape=jax.ShapeDtypeStruct((depth, MOD_ROWS, n), F32),
        compiler_params=_params(("parallel", "parallel"), blocks, _nbytes((d, tn), BF16)),
        name="ada_modulation",
    )(cond, *([w_ada] * parts), b_ada.reshape(depth, 1, n))


def _norm_mod_kernel(*refs):
    x_refs, (g_ref, shift_ref, scale_ref, o_ref) = refs[:-4], refs[-4:]
    gain = g_ref[...] * (1.0 + scale_ref[0])
    shift = shift_ref[0]
    part_rows = x_refs[0].shape[0]

    for part, x_ref in enumerate(x_refs):
        def row_group(i, carry, x_ref=x_ref, part=part):
            start = pl.multiple_of(i * NORM_ROW_GROUP, NORM_ROW_GROUP)
            x = x_ref[pl.ds(start, NORM_ROW_GROUP), :]
            y = x * lax.rsqrt(jnp.mean(x * x, axis=-1, keepdims=True) + NORM_EPS)
            o_ref[pl.ds(part * part_rows + start, NORM_ROW_GROUP), :] = (y * gain + shift).astype(o_ref.dtype)
            return carry

        lax.fori_loop(0, part_rows // NORM_ROW_GROUP, row_group, 0, unroll=NORM_GROUP_UNROLL)


class _ModRows:
    def __init__(self, first, span):
        self.first, self.span = first, span

    def of_tile(self, i, tm):
        assert self.span % tm == 0
        return self.first + (i * tm) // self.span


def _norm_mod(x, g, shift, scale, mod_rows):
    t, d = x.shape
    tm = _tile(mod_rows.span, 256, NORM_ROW_GROUP)
    parts = STREAM_SPLITS if tm % (STREAM_SPLITS * NORM_ROW_GROUP * NORM_GROUP_UNROLL) == 0 else 1
    mod_spec = pl.BlockSpec((1, 1, d), lambda i: (mod_rows.of_tile(i, tm), 0, 0))
    blocks = _nbytes((tm, d), F32) + _nbytes((tm, d), BF16) + 3 * _nbytes((1, d), F32)
    x_specs = [pl.BlockSpec((tm // parts, d), functools.partial(lambda i, p: (i * parts + p, 0), p=p))
               for p in range(parts)]
    return pl.pallas_call(
        _norm_mod_kernel,
        grid=(t // tm,),
        in_specs=x_specs + [pl.BlockSpec((1, d), lambda i: (0, 0)), mod_spec, mod_spec],
        out_specs=pl.BlockSpec((tm, d), lambda i: (i, 0)),
        out_shape=jax.ShapeDtypeStruct((t, d), BF16),
        compiler_params=_params(("parallel",), blocks, 2 * _nbytes((tm, d), F32)),
        name="norm_modulate",
    )(*([x] * parts), g.reshape(1, d), shift, scale)


class _Weight(NamedTuple):
    array: jax.Array
    lead: tuple = ()

    @property
    def shape(self):
        return self.array.shape[-2:]


def _weight_spec(w, k, tn, k_block=0):
    return pl.BlockSpec((None,) * len(w.lead) + (k, tn), lambda i, j: w.lead + (k_block, j))


def _cast_rows(rows, n_steps):
    for r in range(BF16_SUBLANES, rows + 1, BF16_SUBLANES):
        if rows % r == 0 and rows // r <= n_steps:
            return r
    return None


class _SideCasts:
    def __init__(self, weights, grid):
        n_steps, n_inner = grid[0] * grid[1], grid[1]
        self.weights = list(weights)
        self.rows = [_cast_rows(w.shape[0], n_steps) for w in self.weights]
        self.carried = [w for w, r in zip(self.weights, self.rows) if r is not None]
        self.in_specs, self.out_specs, self.out_shapes, self.block_bytes = [], [], [], 0
        for w, r in zip(self.weights, self.rows):
            if r is None:
                continue
            rows, cols = w.shape
            block = functools.partial(lambda i, j, last: jnp.minimum(i * n_inner + j, last), last=rows // r - 1)
            self.in_specs.append(pl.BlockSpec((None,) * len(w.lead) + (r, cols),
                                              functools.partial(lambda i, j, w, block: w.lead + (block(i, j), 0),
                                                                w=w, block=block)))
            self.out_specs.append(pl.BlockSpec((r, cols), functools.partial(lambda i, j, block: (block(i, j), 0),
                                                                            block=block)))
            self.out_shapes.append(jax.ShapeDtypeStruct((rows, cols), BF16))
            self.block_bytes += _nbytes((r, cols), F32) + _nbytes((r, cols), BF16)

    @property
    def n(self):
        return len(self.carried)

    def inputs(self):
        return [w.array for w in self.carried]

    def results(self, outs):
        outs = iter(outs)
        return [_Weight(next(outs)) if r is not None else _Weight(w.array[w.lead].astype(BF16))
                for w, r in zip(self.weights, self.rows)]


def _column_halves(n):
    if n % (2 * MXU_WIDTH):
        return (slice(0, n),)
    return (slice(0, n // 2), slice(n // 2, n))


def _run_side_casts(in_refs, out_refs):
    for src, dst in zip(in_refs, out_refs):
        dst[...] = src[...].astype(dst.dtype)


def _matmul_kernel(a_ref, w_ref, *rest, n_side):
    o_ref = rest[n_side]
    _run_side_casts(rest[:n_side], rest[n_side + 1:])
    for cols in _column_halves(o_ref.shape[1]):
        o_ref[:, cols] = jnp.dot(a_ref[...], w_ref[:, cols], preferred_element_type=F32).astype(o_ref.dtype)


def _matmul(a, w, side_weights=()):
    t, k = a.shape
    n = w.shape[1]
    tm = _tile(t, 1024, 8)
    for tn in (_tile(n, 1024, 128), _tile(n, 512, 128)):
        grid = (t // tm, n // tn)
        side = _SideCasts(side_weights, grid)
        blocks = _nbytes((tm, k), BF16) + _nbytes((k, tn), BF16) + _nbytes((tm, tn), BF16) + side.block_bytes
        if _fits_vmem(blocks, _nbytes((tm, tn), F32)):
            break
    out, *casts = pl.pallas_call(
        functools.partial(_matmul_kernel, n_side=side.n),
        grid=grid,
        in_specs=[pl.BlockSpec((tm, k), lambda i, j: (i, 0)), _weight_spec(w, k, tn)] + side.in_specs,
        out_specs=[pl.BlockSpec((tm, tn), lambda i, j: (i, j))] + side.out_specs,
        out_shape=[jax.ShapeDtypeStruct((t, n), BF16)] + side.out_shapes,
        compiler_params=_params(("arbitrary", "arbitrary"), blocks, _nbytes((tm, tn), F32)),
        name="projection",
    )(a, w.array, *side.inputs())
    return out, side.results(casts)


def _gate_up_kernel(h_ref, wg_ref, wu_ref, *rest, n_side):
    o_ref = rest[n_side]
    _run_side_casts(rest[:n_side], rest[n_side + 1:])
    h = h_ref[...]
    for cols in _column_halves(o_ref.shape[1]):
        g = jnp.dot(h, wg_ref[:, cols], preferred_element_type=F32)
        u = jnp.dot(h, wu_ref[:, cols], preferred_element_type=F32)
        o_ref[:, cols] = (g * jax.nn.sigmoid(g) * u).astype(o_ref.dtype)


def _gate_up(h, wg, wu, side_weights=()):
    t, k = h.shape
    n = wg.shape[1]
    tm, tn = _tile(t, 1024, 8), _tile(n, 512, 128)
    grid = (t // tm, n // tn)
    side = _SideCasts(side_weights, grid)
    blocks = _nbytes((tm, k), BF16) + 2 * _nbytes((k, tn), BF16) + _nbytes((tm, tn), BF16) + side.block_bytes
    out, *casts = pl.pallas_call(
        functools.partial(_gate_up_kernel, n_side=side.n),
        grid=grid,
        in_specs=[pl.BlockSpec((tm, k), lambda i, j: (i, 0)), _weight_spec(wg, k, tn), _weight_spec(wu, k, tn)]
        + side.in_specs,
        out_specs=[pl.BlockSpec((tm, tn), lambda i, j: (i, j))] + side.out_specs,
        out_shape=[jax.ShapeDtypeStruct((t, n), BF16)] + side.out_shapes,
        compiler_params=_params(("arbitrary", "arbitrary"), blocks, 3 * _nbytes((tm, tn), F32)),
        name="ffn_gate_up",
    )(h, wg.array, wu.array, *side.inputs())
    return out, side.results(casts)


def _residual_kernel(*refs, n_parts, coef):
    a_refs, w_refs = refs[:n_parts], refs[n_parts:2 * n_parts]
    x_ref, gate_ref, o_ref = refs[2 * n_parts:]
    for cols in _column_halves(o_ref.shape[1]):
        acc = jnp.dot(a_refs[0][...], w_refs[0][:, cols], preferred_element_type=F32)
        for a_ref, w_ref in zip(a_refs[1:], w_refs[1:]):
            acc += jnp.dot(a_ref[...], w_ref[:, cols], preferred_element_type=F32)
        o_ref[:, cols] = x_ref[:, cols] + (coef * gate_ref[0, :, cols]) * acc


def _residual_matmul(parts, w, x, gate, mod_rows, coef):
    t, n = x.shape
    widths = [p.shape[1] for p in parts]
    assert len(set(widths)) == 1 and sum(widths) == w.shape[0]
    kp = widths[0]
    tm = _tile(mod_rows.span, 1024 if _nbytes((1024, sum(widths)), BF16) <= LHS_BLOCK_BUDGET else 512, 8)
    tn = _tile(n, 512, 128)
    blocks = (len(parts) * (_nbytes((tm, kp), BF16) + _nbytes((kp, tn), BF16))
              + 2 * _nbytes((tm, tn), F32) + _nbytes((1, tn), F32))
    in_specs = [pl.BlockSpec((tm, kp), lambda i, j: (i, 0)) for _ in parts]
    in_specs += [_weight_spec(w, kp, tn, k_block=p) for p in range(len(parts))]
    in_specs += [pl.BlockSpec((tm, tn), lambda i, j: (i, j)),
                 pl.BlockSpec((1, 1, tn), lambda i, j: (mod_rows.of_tile(i, tm), 0, j))]
    return pl.pallas_call(
        functools.partial(_residual_kernel, n_parts=len(parts), coef=coef),
        grid=(t // tm, n // tn),
        in_specs=in_specs,
        out_specs=pl.BlockSpec((tm, tn), lambda i, j: (i, j)),
        out_shape=jax.ShapeDtypeStruct((t, n), F32),
        compiler_params=_params(("parallel", "parallel"), blocks, _nbytes((tm, tn), F32)),
        name="residual_matmul",
    )(*parts, *([w.array] * len(parts)), x, gate)


def _gelu_tanh(x):
    return x * (0.5 * (1.0 + jnp.tanh(math.sqrt(2.0 / math.pi) * (x + 0.044715 * (x * x * x)))))


def _gmlp_kernel(uv_ref, ws_ref, bs_ref, vg_ref, o_ref, *, n_groups, n_chunks):
    for g in range(n_groups):
        u_cols = slice(g * HEAD_DIM, (g + 1) * HEAD_DIM)
        v_cols = slice((n_groups + g) * HEAD_DIM, (n_groups + g + 1) * HEAD_DIM)
        w = ws_ref[g]
        b = bs_ref[g]
        vg = vg_ref[:, u_cols]

        for c in range(n_chunks):
            rows = slice(c * CHUNK, (c + 1) * CHUNK)
            u = _gelu_tanh(uv_ref[rows, u_cols].astype(F32))
            v = _gelu_tanh(uv_ref[rows, v_cols].astype(F32))
            v = v - jnp.mean(v, axis=-1, keepdims=True)
            vn = (v * lax.rsqrt(jnp.mean(v * v, axis=-1, keepdims=True) + NORM_EPS)) * vg
            s = jnp.dot(w, vn.astype(BF16), preferred_element_type=F32) + b
            o_ref[rows, u_cols] = (u * s).astype(o_ref.dtype)


def _gmlp(proj, w_s, b_s, vnorm_g):
    t = proj.shape[0]
    n_groups = w_s.shape[0]
    gm = n_groups * HEAD_DIM
    tc = _tile(t, 512, CHUNK)
    blocks = (_nbytes((tc, 2 * gm), BF16) + _nbytes((tc, gm), BF16) + _nbytes(w_s.shape, BF16)
              + _nbytes((n_groups, CHUNK, 128), F32) + _nbytes((8, gm), F32))
    return pl.pallas_call(
        functools.partial(_gmlp_kernel, n_groups=n_groups, n_chunks=tc // CHUNK),
        grid=(t // tc,),
        in_specs=[pl.BlockSpec((tc, 2 * gm), lambda i: (i, 0)),
                  pl.BlockSpec(w_s.shape, lambda i: (0, 0, 0)),
                  pl.BlockSpec((n_groups, CHUNK, 1), lambda i: (0, 0, 0)),
                  pl.BlockSpec((1, gm), lambda i: (0, 0))],
        out_specs=pl.BlockSpec((tc, gm), lambda i: (i, 0)),
        out_shape=jax.ShapeDtypeStruct((t, gm), BF16),
        compiler_params=_params(("parallel",), blocks, 16 * _nbytes((CHUNK, HEAD_DIM), F32)),
        name="gmlp_chunk_gating",
    )(proj, w_s.astype(BF16), b_s.reshape(n_groups, CHUNK, 1), vnorm_g.reshape(1, gm))


def _swap_rope_halves(x):
    lane = lax.broadcasted_iota(jnp.int32, x.shape, x.ndim - 1)
    return jnp.where(lane % 64 < 32, pltpu.roll(x, HEAD_DIM - 32, x.ndim - 1), pltpu.roll(x, 32, x.ndim - 1))


def _rms_head(x, g):
    x = x.astype(F32)
    return (x * lax.rsqrt(jnp.mean(x * x, axis=-1, keepdims=True) + NORM_EPS)) * g


def _head_norm_kernel(x_ref, g_ref, *rest, rope, heads):
    o_ref = rest[-1]
    for h in range(heads):
        cols = slice(h * HEAD_DIM, (h + 1) * HEAD_DIM)
        y = _rms_head(x_ref[:, cols], g_ref[:, cols])
        if rope:
            cos_ref, sin_ref = rest[:2]
            y = y * cos_ref[...] + _swap_rope_halves(y) * sin_ref[...]
        o_ref[:, cols] = y.astype(o_ref.dtype)


def _head_norm(src, first_head, gains, rope_tables=None):
    t = src.shape[0]
    n_heads = gains.shape[1] // HEAD_DIM
    heads = HEAD_NORM_GROUP if first_head % HEAD_NORM_GROUP == 0 and n_heads % HEAD_NORM_GROUP == 0 else 1
    width = heads * HEAD_DIM
    seq = rope_tables[0].shape[0] if rope_tables is not None else t
    tm = _tile(seq, 1024, 16)
    per_seq = seq // tm
    in_specs = [pl.BlockSpec((tm, width), lambda i, h: (i, first_head // heads + h)),
                pl.BlockSpec((1, width), lambda i, h: (0, h))]
    args = [src, gains]
    if rope_tables is not None:
        in_specs += [pl.BlockSpec((tm, HEAD_DIM), lambda i, h: (i % per_seq, 0))] * 2
        args += list(rope_tables)
    blocks = 2 * _nbytes((tm, width), BF16) + 2 * _nbytes((tm, HEAD_DIM), F32)
    return pl.pallas_call(
        functools.partial(_head_norm_kernel, rope=rope_tables is not None, heads=heads),
        grid=(t // tm, n_heads // heads),
        in_specs=in_specs,
        out_specs=pl.BlockSpec((tm, width), lambda i, h: (i, h)),
        out_shape=jax.ShapeDtypeStruct((t, n_heads * HEAD_DIM), BF16),
        compiler_params=_params(("parallel", "parallel"), blocks, 6 * heads * _nbytes((tm, HEAD_DIM), F32)),
        name="head_norm_rope" if rope_tables is not None else "head_norm",
    )(*args)


def _rope_tables(seq):
    half = HEAD_DIM // 2
    t = jnp.arange(seq)
    rows = (t // GRID_W).astype(F32)
    cols = (t % GRID_W).astype(F32)
    inv = 1.0 / (ROPE_THETA ** (jnp.arange(0, half, 2, dtype=F32) / half))
    ang = jnp.concatenate([rows[:, None] * inv, rows[:, None] * inv, cols[:, None] * inv, cols[:, None] * inv], -1)
    sign = jnp.where(jnp.arange(HEAD_DIM) % half < half // 2, -1.0, 1.0).astype(F32)
    return jnp.cos(ang), jnp.sin(ang) * sign


def _nt_dot(a, b):
    return lax.dot_general(a, b, (((1,), (1,)), ((), ())), preferred_element_type=F32)


def _sink_column(sink_ref, first, group, rows_per_head):
    head = lax.broadcasted_iota(jnp.int32, (group * rows_per_head, 1), 0) // rows_per_head
    col = jnp.zeros((group * rows_per_head, 1), F32)
    for g in range(group):
        col = jnp.where(head == g, sink_ref[first + g], col)
    return col


def _swa_kernel(sink_ref, q_ref, k_ref, v_ref, kc_ref, vc_ref, o_ref, *, tq, seq, group):
    kv_head, q_tile = pl.program_id(1), pl.program_id(2)
    kc, vc = kc_ref[0], vc_ref[0]
    sink = _sink_column(sink_ref, kv_head * group, group, BLOCK)
    q_off = lax.broadcasted_iota(jnp.int32, (group * BLOCK, 1), 0) % BLOCK
    k_off = lax.broadcasted_iota(jnp.int32, (1, 3 * BLOCK), 1)

    def scores(j):
        base = q_tile * tq + j * BLOCK
        start = pl.multiple_of(jnp.clip(base - BLOCK, 0, seq - 3 * BLOCK), BLOCK)
        q_rows = pl.ds(pl.multiple_of(j * BLOCK, BLOCK), BLOCK)
        q = jnp.concatenate([q_ref[0, q_rows, g * HEAD_DIM:(g + 1) * HEAD_DIM] for g in range(group)], axis=0)
        s_w = _nt_dot(q, k_ref[0, pl.ds(start, 3 * BLOCK), :])
        return base, start, q_rows, s_w, _nt_dot(q, kc)

    def attend(base, start, q_rows, s_w, s_c):
        s_w = jnp.where(jnp.abs((start + k_off) - (base + q_off)) <= WINDOW, s_w, NEG_INF)
        m = jnp.maximum(jnp.maximum(jnp.max(s_w, -1, keepdims=True), jnp.max(s_c, -1, keepdims=True)), sink)
        p_w = jnp.exp(s_w - m)
        p_c = jnp.exp(s_c - m)
        den = jnp.sum(p_w, -1, keepdims=True) + jnp.sum(p_c, -1, keepdims=True) + jnp.exp(sink - m)
        o = (jnp.dot(p_w.astype(BF16), v_ref[0, pl.ds(start, 3 * BLOCK), :], preferred_element_type=F32)
             + jnp.dot(p_c.astype(BF16), vc, preferred_element_type=F32)) / den
        for g in range(group):
            o_ref[0, q_rows, g * HEAD_DIM:(g + 1) * HEAD_DIM] = o[g * BLOCK:(g + 1) * BLOCK].astype(o_ref.dtype)

    def block_group(i, carry):
        staged = [scores(i * SWA_BLOCK_UNROLL + u) for u in range(SWA_BLOCK_UNROLL)]
        for block_state in staged:
            attend(*block_state)
        return carry

    assert (tq // BLOCK) % SWA_BLOCK_UNROLL == 0
    lax.fori_loop(0, tq // BLOCK // SWA_BLOCK_UNROLL, block_group, 0)


def _swa(qk, proj, kc, projc, sink, *, n_heads, v_col, batch, seq):
    group = n_heads // SWA_KV_HEADS
    lc = kc.shape[0] // batch
    gw = group * HEAD_DIM
    tq = _tile(seq, 1024, BLOCK)
    qk3 = qk.reshape(batch, seq, qk.shape[1])
    proj3 = proj.reshape(batch, seq, proj.shape[1])
    kc3 = kc.reshape(batch, lc, kc.shape[1])
    projc3 = projc.reshape(batch, lc, projc.shape[1])
    blocks = (2 * _nbytes((tq, gw), BF16) + 2 * _nbytes((seq, HEAD_DIM), BF16) + 2 * _nbytes((lc, HEAD_DIM), BF16))
    temps = 6 * _nbytes((group * BLOCK, 3 * BLOCK + lc), F32)
    out = pl.pallas_call(
        functools.partial(_swa_kernel, tq=tq, seq=seq, group=group),
        grid=(batch, SWA_KV_HEADS, seq // tq),
        in_specs=[pl.BlockSpec(memory_space=pltpu.SMEM),
                  pl.BlockSpec((1, tq, gw), lambda b, h, i: (b, i, h)),
                  pl.BlockSpec((1, seq, HEAD_DIM), lambda b, h, i: (b, 0, n_heads + h)),
                  pl.BlockSpec((1, seq, HEAD_DIM), lambda b, h, i: (b, 0, v_col + h)),
                  pl.BlockSpec((1, lc, HEAD_DIM), lambda b, h, i: (b, 0, h)),
                  pl.BlockSpec((1, lc, HEAD_DIM), lambda b, h, i: (b, 0, v_col + h))],
        out_specs=pl.BlockSpec((1, tq, gw), lambda b, h, i: (b, i, h)),
        out_shape=jax.ShapeDtypeStruct((batch, seq, n_heads * HEAD_DIM), BF16),
        compiler_params=_params(("parallel", "parallel", "parallel"), blocks, temps),
        name="window_attention",
    )(sink, qk3, qk3, proj3, kc3, projc3)
    return out.reshape(batch * seq, n_heads * HEAD_DIM)


def _ctx_attn_kernel(*refs, group, lc, use_sink):
    if use_sink:
        sink_ref, q_ref, k_ref, v_ref, o_ref = refs
    else:
        q_ref, k_ref, v_ref, o_ref = refs
    q = jnp.concatenate([q_ref[0, :, g * HEAD_DIM:(g + 1) * HEAD_DIM] for g in range(group)], axis=0)
    s = _nt_dot(q, k_ref[0])
    m = jnp.max(s, -1, keepdims=True)
    if use_sink:
        sink = _sink_column(sink_ref, pl.program_id(1) * group, group, lc)
        m = jnp.maximum(m, sink)
    p = jnp.exp(s - m)
    den = jnp.sum(p, -1, keepdims=True)
    if use_sink:
        den = den + jnp.exp(sink - m)
    o = jnp.dot(p.astype(BF16), v_ref[0], preferred_element_type=F32) / den
    for g in range(group):
        o_ref[0, :, g * HEAD_DIM:(g + 1) * HEAD_DIM] = o[g * lc:(g + 1) * lc].astype(o_ref.dtype)


def _ctx_attn(qc, kc, projc, sink, *, n_heads, n_kv, v_col, batch):
    group = n_heads // n_kv
    lc = qc.shape[0] // batch
    gw = group * HEAD_DIM
    qc3 = qc.reshape(batch, lc, qc.shape[1])
    kc3 = kc.reshape(batch, lc, kc.shape[1])
    projc3 = projc.reshape(batch, lc, projc.shape[1])
    use_sink = sink is not None
    in_specs = [pl.BlockSpec((1, lc, gw), lambda b, h: (b, 0, h)),
                pl.BlockSpec((1, lc, HEAD_DIM), lambda b, h: (b, 0, h)),
                pl.BlockSpec((1, lc, HEAD_DIM), lambda b, h: (b, 0, v_col + h))]
    args = [qc3, kc3, projc3]
    if use_sink:
        in_specs = [pl.BlockSpec(memory_space=pltpu.SMEM)] + in_specs
        args = [sink] + args
    blocks = 2 * _nbytes((lc, gw), BF16) + 2 * _nbytes((lc, HEAD_DIM), BF16)
    out = pl.pallas_call(
        functools.partial(_ctx_attn_kernel, group=group, lc=lc, use_sink=use_sink),
        grid=(batch, n_kv),
        in_specs=in_specs,
        out_specs=pl.BlockSpec((1, lc, gw), lambda b, h: (b, 0, h)),
        out_shape=jax.ShapeDtypeStruct((batch, lc, n_heads * HEAD_DIM), BF16),
        compiler_params=_params(("parallel", "parallel"), blocks, 6 * _nbytes((group * lc, lc), F32)),
        name="context_attention",
    )(*args)
    return out.reshape(batch * lc, n_heads * HEAD_DIM)


def _nat_bias_kernel(rpb_ref, o_ref):
    h = pl.program_id(0)
    n_dc = 2 * NA_KW - 1
    w = lax.broadcasted_iota(jnp.int32, (GRID_W, 2 * GRID_W), 0)
    lane = lax.broadcasted_iota(jnp.int32, (GRID_W, 2 * GRID_W), 1)
    second = lane >= GRID_W
    kc = lane % GRID_W
    c0 = jnp.clip(w - NA_KW // 2, 0, GRID_W - NA_KW)
    dc = jnp.where((kc >= c0) & (kc < c0 + NA_KW), kc - w + (NA_KW - 1), -1)

    def pair(dr, carry):
        base = (h * (2 * NA_KH - 1) + dr) * n_dc
        tile = jnp.full((GRID_W, 2 * GRID_W), NEG_INF, F32)
        for j in range(n_dc):
            tile = jnp.where(dc == j, jnp.where(second, rpb_ref[base + n_dc + j], rpb_ref[base + j]), tile)
        o_ref[0, dr] = tile
        return carry

    lax.fori_loop(0, 2 * NA_KH - 2, pair, 0)


def _nat_bias(rpb):
    n_heads = rpb.shape[0]
    assert rpb.shape[1:] == (2 * NA_KH - 1, 2 * NA_KW - 1)
    shape = (n_heads, 2 * NA_KH - 2, GRID_W, 2 * GRID_W)
    return pl.pallas_call(
        _nat_bias_kernel,
        grid=(n_heads,),
        in_specs=[pl.BlockSpec(memory_space=pltpu.SMEM)],
        out_specs=pl.BlockSpec((1,) + shape[1:], lambda h: (h, 0, 0, 0)),
        out_shape=jax.ShapeDtypeStruct(shape, F32),
        compiler_params=_params(("parallel",), _nbytes(shape[1:], F32)),
        name="nat_bias_tiles",
    )(rpb.reshape(-1))


def _nat_kernel(q_ref, k_ref, v_ref, kc_ref, vc_ref, gq_ref, gk_ref, bias_ref, o_ref, kn_ref, *, rows, norm_rows,
                heads):
    n_pairs = NA_KH // 2
    gq = gq_ref[...] * ATTN_SCALE
    gk = gk_ref[...]
    n_ctx = kc_ref.shape[1] // HEAD_DIM

    for head in range(heads):
        cols = slice(head * HEAD_DIM, (head + 1) * HEAD_DIM)

        def norm_keys(i, carry, cols=cols):
            chunk = pl.ds(pl.multiple_of(i * norm_rows, norm_rows), norm_rows)
            kn_ref[chunk, :] = _rms_head(k_ref[0, chunk, cols], gk).astype(BF16)
            return carry

        lax.fori_loop(0, rows * GRID_W // norm_rows, norm_keys, 0, unroll=2)
        kc = _rms_head(kc_ref[0, :, cols], gk).astype(BF16)
        vc = vc_ref[0, :, cols]

        def window(r):
            r0 = jnp.clip(r - NA_KH // 2, 0, rows - NA_KH)
            q_rows = pl.ds(pl.multiple_of(r * GRID_W, GRID_W), GRID_W)
            k_rows = pl.ds(pl.multiple_of(r0 * GRID_W, GRID_W), NA_KH * GRID_W)
            return q_rows, k_rows, r0 - r + (NA_KH - 1)

        def logits(dr0, s_nb, s_c, head=head):
            s = [s_nb[:, 2 * GRID_W * j:2 * GRID_W * (j + 1)] + bias_ref[head, dr0 + 2 * j] for j in range(n_pairs)]
            s += [s_c[:, HEAD_DIM * j:HEAD_DIM * (j + 1)] for j in range(n_ctx)]
            return s, jnp.max(functools.reduce(jnp.maximum, s), -1, keepdims=True)

        def row_group(i, carry, cols=cols, kc=kc, vc=vc, window=window, logits=logits):
            wins = [window(i * NAT_ROW_UNROLL + u) for u in range(NAT_ROW_UNROLL)]
            q = [_rms_head(q_ref[0, q_rows, cols], gq).astype(BF16) for q_rows, _, _ in wins]
            s_c = _nt_dot(jnp.concatenate(q, axis=0), kc)
            s_nb = [_nt_dot(qu, kn_ref[k_rows, :]) for qu, (_, k_rows, _) in zip(q, wins)]
            staged = [logits(dr0, s_nb[u], s_c[u * GRID_W:(u + 1) * GRID_W]) for u, (_, _, dr0) in enumerate(wins)]
            p = [[jnp.exp(sj - m) for sj in s] for s, m in staged]
            den = [jnp.sum(functools.reduce(jnp.add, pu), -1, keepdims=True) for pu in p]
            o_nb = [jnp.dot(jnp.concatenate(pu[:n_pairs], axis=-1).astype(BF16), v_ref[0, k_rows, cols],
                            preferred_element_type=F32) for pu, (_, k_rows, _) in zip(p, wins)]
            p_c = [jnp.concatenate(pu[n_pairs:], axis=-1).astype(BF16) for pu in p]
            o_c = jnp.dot(jnp.concatenate(p_c, axis=0), vc, preferred_element_type=F32)
            for u, (q_rows, _, _) in enumerate(wins):
                o = (o_nb[u] + o_c[u * GRID_W:(u + 1) * GRID_W]) / den[u]
                o_ref[0, q_rows, cols] = o.astype(o_ref.dtype)
            return carry

        lax.fori_loop(0, rows // NAT_ROW_UNROLL, row_group, 0)


def _nat(proj, projc, gq, gk, bias, *, n_heads, batch, seq):
    rows = seq // GRID_W
    assert seq % GRID_W == 0 and rows >= NA_KH and rows % NAT_ROW_UNROLL == 0
    lc = projc.shape[0] // batch
    assert lc % HEAD_DIM == 0
    heads = NAT_HEADS_PER_STEP if n_heads % NAT_HEADS_PER_STEP == 0 else 1
    width = heads * HEAD_DIM
    norm_rows = _tile(seq, 512, 16)
    proj3 = proj.reshape(batch, seq, proj.shape[1])
    projc3 = projc.reshape(batch, lc, projc.shape[1])

    def head_block(n, first_head):
        return pl.BlockSpec((1, n, width), lambda b, h: (b, 0, first_head // heads + h))

    gain_block = pl.BlockSpec((1, HEAD_DIM), lambda b, h: (0, 0))
    bias_block = (heads,) + bias.shape[1:]
    blocks = 4 * _nbytes((seq, width), BF16) + 2 * _nbytes((lc, width), BF16) + _nbytes(bias_block, F32)
    temps = (_nbytes((seq, HEAD_DIM), BF16)
             + 4 * NAT_ROW_UNROLL * _nbytes((GRID_W, NA_KH * GRID_W + lc), F32) + 4 * _nbytes((norm_rows, HEAD_DIM), F32))
    out = pl.pallas_call(
        functools.partial(_nat_kernel, rows=rows, norm_rows=norm_rows, heads=heads),
        grid=(batch, n_heads // heads),
        in_specs=[head_block(seq, 0), head_block(seq, n_heads), head_block(seq, 2 * n_heads),
                  head_block(lc, n_heads), head_block(lc, 2 * n_heads),
                  gain_block, gain_block,
                  pl.BlockSpec(bias_block, lambda b, h: (h, 0, 0, 0))],
        out_specs=head_block(seq, 0),
        out_shape=jax.ShapeDtypeStruct((batch, seq, n_heads * HEAD_DIM), BF16),
        scratch_shapes=[pltpu.VMEM((seq, HEAD_DIM), BF16)],
        compiler_params=_params(("parallel", "parallel"), blocks, temps),
        name="neighbourhood_attention",
    )(proj3, proj3, proj3, projc3, projc3, gq.reshape(1, HEAD_DIM), gk.reshape(1, HEAD_DIM), bias)
    return out.reshape(batch * seq, n_heads * HEAD_DIM)


def _gains(*groups):
    return jnp.concatenate([jnp.tile(g.astype(F32), n) for g, n in groups]).reshape(1, -1)


def kernel(x, c, ctx, c_ctx, w_ada, b_ada, norm_g, ffn_w_gate, ffn_w_up, ffn_w_down, mix_ab_w_in, mix_ab_w_out,
           gm_vnorm_g, gm_w_s, gm_b_s, swa_qnorm_g, swa_knorm_g, swa_sink, nat_w_qkv, nat_w_out, nat_qnorm_g,
           nat_knorm_g, nat_rpb):
    batch, seq, d = x.shape
    lc = ctx.shape[1]
    depth = w_ada.shape[0]
    ctx_row = batch
    assert batch + 1 <= MOD_ROWS

    cond = jnp.zeros((MOD_ROWS, d), F32).at[:batch].set(c).at[ctx_row].set(c_ctx)
    mod = _ada(cond, w_ada, b_ada).reshape(depth, MOD_ROWS, N_MOD, 1, d)

    lat_row = _ModRows(0, seq)
    ctx_row_of = _ModRows(ctx_row, batch * lc)

    def mod_vec(layer, sub, which):
        return mod[layer, :, 3 * sub + which]

    def ffn_in(xs, row, layer, sub):
        return _norm_mod(xs, norm_g[layer, sub], mod_vec(layer, sub, 0), mod_vec(layer, sub, 1), row)

    def ffn_out(a, wd, xs, row, layer, sub):
        return _residual_matmul([a], wd, xs, mod_vec(layer, sub, 2), row, HALF_STEP)

    def ffn_ctx(cs, layer, sub, wg, wu, wd):
        a, _ = _gate_up(ffn_in(cs, ctx_row_of, layer, sub), wg, wu)
        return ffn_out(a, wd, cs, ctx_row_of, layer, sub)

    def ffn_f32(layer, idx):
        return (_Weight(w, (layer, idx)) for w in (ffn_w_gate, ffn_w_up, ffn_w_down))

    xs = x.reshape(batch * seq, d)
    cs = ctx.reshape(batch * lc, d)
    rope = _rope_tables(seq)

    wg0_f32, wu0_f32, _ = ffn_f32(0, 0)
    wg, wu = (_Weight(w.array[w.lead].astype(BF16)) for w in (wg0_f32, wu0_f32))

    for layer in range(depth):
        last = layer == depth - 1
        j = layer // 2
        even = layer % 2 == 0
        mix_in_f32 = _Weight(mix_ab_w_in if even else nat_w_qkv, (j,))
        mix_out_f32 = _Weight(mix_ab_w_out if even else nat_w_out, (j,))
        _, _, wd_f32 = ffn_f32(layer, 0)
        a, (wd, mix_in, mix_out) = _gate_up(ffn_in(xs, lat_row, layer, 0), wg, wu, [wd_f32, mix_in_f32, mix_out_f32])
        xs = ffn_out(a, wd, xs, lat_row, layer, 0)
        cs = ffn_ctx(cs, layer, 0, wg, wu, wd)
        h = _norm_mod(xs, norm_g[layer, 1], mod_vec(layer, 1, 0), mod_vec(layer, 1, 1), lat_row)
        hc = _norm_mod(cs, norm_g[layer, 1], mod_vec(layer, 1, 0), mod_vec(layer, 1, 1), ctx_row_of)
        gate = mod_vec(layer, 1, 2)
        wg1_f32, wu1_f32, wd1_f32 = ffn_f32(layer, 1)
        proj, (wg, wu) = _matmul(h, mix_in, [wg1_f32, wu1_f32])
        projc, _ = _matmul(hc, mix_in)
        if even:
            n_groups = gm_w_s.shape[1]
            n_heads = swa_sink.shape[1]
            assert n_groups == n_heads
            q_col = 2 * n_groups
            v_col = q_col + n_heads + SWA_KV_HEADS
            qk_gain = _gains((swa_qnorm_g[j] * ATTN_SCALE, n_heads), (swa_knorm_g[j], SWA_KV_HEADS))
            qk = _head_norm(proj, q_col, qk_gain, rope)
            qkc = _head_norm(projc, q_col, qk_gain)
            kc = qkc[:, n_heads * HEAD_DIM:]
            y_g = _gmlp(proj, gm_w_s[j], gm_b_s[j], gm_vnorm_g[j])
            y_a = _swa(qk, proj, kc, projc, swa_sink[j], n_heads=n_heads, v_col=v_col, batch=batch, seq=seq)
            xs = _residual_matmul([y_g, y_a], mix_out, xs, gate, lat_row, 1.0)
            if not last:
                yc_g = _gmlp(projc, gm_w_s[j], gm_b_s[j], gm_vnorm_g[j])
                yc_a = _ctx_attn(qkc, kc, projc, swa_sink[j], n_heads=n_heads, n_kv=SWA_KV_HEADS, v_col=v_col,
                                 batch=batch)
                cs = _residual_matmul([yc_g, yc_a], mix_out, cs, gate, ctx_row_of, 1.0)
        else:
            n_heads = nat_rpb.shape[1]
            y = _nat(proj, projc, nat_qnorm_g[j], nat_knorm_g[j], _nat_bias(nat_rpb[j]), n_heads=n_heads,
                     batch=batch, seq=seq)
            xs = _residual_matmul([y], mix_out, xs, gate, lat_row, 1.0)
            if not last:
                qk_gain = _gains((nat_qnorm_g[j] * ATTN_SCALE, n_heads), (nat_knorm_g[j], n_heads))
                qkc = _head_norm(projc, 0, qk_gain)
                yc = _ctx_attn(qkc, qkc[:, n_heads * HEAD_DIM:], projc, None, n_heads=n_heads, n_kv=n_heads,
                               v_col=2 * n_heads, batch=batch)
                cs = _residual_matmul([yc], mix_out, cs, gate, ctx_row_of, 1.0)
        next_f32 = [] if last else list(ffn_f32(layer + 1, 0))[:2]
        a, (wd1, *next_w) = _gate_up(ffn_in(xs, lat_row, layer, 2), wg, wu, [wd1_f32] + next_f32)
        xs = ffn_out(a, wd1, xs, lat_row, layer, 2)
        if not last:
            cs = ffn_ctx(cs, layer, 2, wg, wu, wd1)
            wg, wu = next_w
    return xs.reshape(batch, seq, d)
```

```python
import functools
import math
from typing import NamedTuple

import jax
import jax.numpy as jnp
from jax import lax
from jax.experimental import pallas as pl
from jax.experimental.pallas import tpu as pltpu

F32 = jnp.float32
BF16 = jnp.bfloat16

HEAD_DIM = 128
GRID_W = 64
CHUNK = 128
WINDOW = 128
BLOCK = 128
SWA_KV_HEADS = 4
NA_KH = 8
NA_KW = 16
NAT_ROW_UNROLL = 32
NAT_HEADS_PER_STEP = 1
SWA_BLOCK_UNROLL = 4
HEAD_NORM_GROUP = 4
N_MOD = 9
HALF_STEP = 0.5
ROPE_THETA = 10000.0
NORM_EPS = 1e-6
NEG_INF = -1e30
ATTN_SCALE = HEAD_DIM ** -0.5

V7X_VMEM_BYTES = 64 * 1024 * 1024
VMEM_CEILING = V7X_VMEM_BYTES - 8 * 1024 * 1024
MOD_ROWS = 8
BF16_SUBLANES = 16
STREAM_SPLITS = 4
MXU_WIDTH = 256
NORM_ROW_GROUP = BF16_SUBLANES
NORM_GROUP_UNROLL = 4
LHS_BLOCK_BUDGET = 8 * 1024 * 1024


def _vmem_limit(pipelined_bytes, temp_bytes=0):
    need = int((2 * pipelined_bytes + temp_bytes) * 1.25) + (2 << 20)
    return min(max(need, 16 << 20), VMEM_CEILING)


def _fits_vmem(pipelined_bytes, temp_bytes=0):
    return 2 * pipelined_bytes + temp_bytes <= VMEM_CEILING * 9 // 10


def _params(semantics, pipelined_bytes, temp_bytes=0):
    return pltpu.CompilerParams(dimension_semantics=semantics,
                                vmem_limit_bytes=_vmem_limit(pipelined_bytes, temp_bytes))


def _tile(n, pref, quantum):
    if n <= pref:
        return n
    t = (pref // quantum) * quantum
    while t > quantum and n % t:
        t -= quantum
    assert n % t == 0, (n, pref, quantum)
    return t


def _nbytes(shape, dtype):
    return math.prod(shape) * jnp.dtype(dtype).itemsize


def _ada_kernel(c_ref, *refs):
    w_refs, (b_ref, o_ref) = refs[:-2], refs[-2:]
    c = c_ref[...]
    s = (c * jax.nn.sigmoid(c)).astype(BF16)
    k = w_refs[0].shape[1]
    acc = b_ref[0]
    for part, w_ref in enumerate(w_refs):
        acc = acc + jnp.dot(s[:, part * k:(part + 1) * k], w_ref[0].astype(BF16), preferred_element_type=F32)
    o_ref[0] = acc


def _ada(cond, w_ada, b_ada):
    depth, d, n = w_ada.shape
    tn = _tile(n, 512, 128)
    parts = STREAM_SPLITS if d % (STREAM_SPLITS * HEAD_DIM) == 0 else 1
    k = d // parts
    blocks = _nbytes((MOD_ROWS, d), F32) + _nbytes((d, tn), F32) + 2 * _nbytes((MOD_ROWS, tn), F32)
    w_specs = [pl.BlockSpec((1, k, tn), functools.partial(lambda l, j, p: (l, p, j), p=p)) for p in range(parts)]
    return pl.pallas_call(
        _ada_kernel,
        grid=(depth, n // tn),
        in_specs=[pl.BlockSpec((MOD_ROWS, d), lambda l, j: (0, 0))] + w_specs
        + [pl.BlockSpec((1, 1, tn), lambda l, j: (l, 0, j))],
        out_specs=pl.BlockSpec((1, MOD_ROWS, tn), lambda l, j: (l, 0, j)),
        out_shape=jax.ShapeDtypeStruct((depth, MOD_ROWS, n), F32),
        compiler_params=_params(("parallel", "parallel"), blocks, _nbytes((d, tn), BF16)),
        name="ada_modulation",
    )(cond, *([w_ada] * parts), b_ada.reshape(depth, 1, n))


def _norm_mod_kernel(*refs):
    x_refs, (g_ref, shift_ref, scale_ref, o_ref) = refs[:-4], refs[-4:]
    gain = g_ref[...] * (1.0 + scale_ref[0])
    shift = shift_ref[0]
    part_rows = x_refs[0].shape[0]

    for part, x_ref in enumerate(x_refs):
        def row_group(i, carry, x_ref=x_ref, part=part):
            start = pl.multiple_of(i * NORM_ROW_GROUP, NORM_ROW_GROUP)
            x = x_ref[pl.ds(start, NORM_ROW_GROUP), :]
            y = x * lax.rsqrt(jnp.mean(x * x, axis=-1, keepdims=True) + NORM_EPS)
            o_ref[pl.ds(part * part_rows + start, NORM_ROW_GROUP), :] = (y * gain + shift).astype(o_ref.dtype)
            return carry

        lax.fori_loop(0, part_rows // NORM_ROW_GROUP, row_group, 0, unroll=NORM_GROUP_UNROLL)


class _ModRows:
    def __init__(self, first, span):
        self.first, self.span = first, span

    def of_tile(self, i, tm):
        assert self.span % tm == 0
        return self.first + (i * tm) // self.span


def _norm_mod(x, g, shift, scale, mod_rows):
    t, d = x.shape
    tm = _tile(mod_rows.span, 256, NORM_ROW_GROUP)
    parts = STREAM_SPLITS if tm % (STREAM_SPLITS * NORM_ROW_GROUP * NORM_GROUP_UNROLL) == 0 else 1
    mod_spec = pl.BlockSpec((1, 1, d), lambda i: (mod_rows.of_tile(i, tm), 0, 0))
    blocks = _nbytes((tm, d), F32) + _nbytes((tm, d), BF16) + 3 * _nbytes((1, d), F32)
    x_specs = [pl.BlockSpec((tm // parts, d), functools.partial(lambda i, p: (i * parts + p, 0), p=p))
               for p in range(parts)]
    return pl.pallas_call(
        _norm_mod_kernel,
        grid=(t // tm,),
        in_specs=x_specs + [pl.BlockSpec((1, d), lambda i: (0, 0)), mod_spec, mod_spec],
        out_specs=pl.BlockSpec((tm, d), lambda i: (i, 0)),
        out_shape=jax.ShapeDtypeStruct((t, d), BF16),
        compiler_params=_params(("parallel",), blocks, 2 * _nbytes((tm, d), F32)),
        name="norm_modulate",
    )(*([x] * parts), g.reshape(1, d), shift, scale)


class _Weight(NamedTuple):
    array: jax.Array
    lead: tuple = ()

    @property
    def shape(self):
        return self.array.shape[-2:]


def _weight_spec(w, k, tn, k_block=0):
    return pl.BlockSpec((None,) * len(w.lead) + (k, tn), lambda i, j: w.lead + (k_block, j))


def _cast_rows(rows, n_steps):
    for r in range(BF16_SUBLANES, rows + 1, BF16_SUBLANES):
        if rows % r == 0 and rows // r <= n_steps:
            return r
    return None


class _SideCasts:
    def __init__(self, weights, grid):
        n_steps, n_inner = grid[0] * grid[1], grid[1]
        self.weights = list(weights)
        self.rows = [_cast_rows(w.shape[0], n_steps) for w in self.weights]
        self.carried = [w for w, r in zip(self.weights, self.rows) if r is not None]
        self.in_specs, self.out_specs, self.out_shapes, self.block_bytes = [], [], [], 0
        for w, r in zip(self.weights, self.rows):
            if r is None:
                continue
            rows, cols = w.shape
            block = functools.partial(lambda i, j, last: jnp.minimum(i * n_inner + j, last), last=rows // r - 1)
            self.in_specs.append(pl.BlockSpec((None,) * len(w.lead) + (r, cols),
                                              functools.partial(lambda i, j, w, block: w.lead + (block(i, j), 0),
                                                                w=w, block=block)))
            self.out_specs.append(pl.BlockSpec((r, cols), functools.partial(lambda i, j, block: (block(i, j), 0),
                                                                            block=block)))
            self.out_shapes.append(jax.ShapeDtypeStruct((rows, cols), BF16))
            self.block_bytes += _nbytes((r, cols), F32) + _nbytes((r, cols), BF16)

    @property
    def n(self):
        return len(self.carried)

    def inputs(self):
        return [w.array for w in self.carried]

    def results(self, outs):
        outs = iter(outs)
        return [_Weight(next(outs)) if r is not None else _Weight(w.array[w.lead].astype(BF16))
                for w, r in zip(self.weights, self.rows)]


def _column_halves(n):
    if n % (2 * MXU_WIDTH):
        return (slice(0, n),)
    return (slice(0, n // 2), slice(n // 2, n))


def _run_side_casts(in_refs, out_refs):
    for src, dst in zip(in_refs, out_refs):
        dst[...] = src[...].astype(dst.dtype)


def _matmul_kernel(a_ref, w_ref, *rest, n_side):
    o_ref = rest[n_side]
    _run_side_casts(rest[:n_side], rest[n_side + 1:])
    for cols in _column_halves(o_ref.shape[1]):
        o_ref[:, cols] = jnp.dot(a_ref[...], w_ref[:, cols], preferred_element_type=F32).astype(o_ref.dtype)


def _matmul(a, w, side_weights=()):
    t, k = a.shape
    n = w.shape[1]
    tm = _tile(t, 1024, 8)
    for tn in (_tile(n, 1024, 128), _tile(n, 512, 128)):
        grid = (t // tm, n // tn)
        side = _SideCasts(side_weights, grid)
        blocks = _nbytes((tm, k), BF16) + _nbytes((k, tn), BF16) + _nbytes((tm, tn), BF16) + side.block_bytes
        if _fits_vmem(blocks, _nbytes((tm, tn), F32)):
            break
    out, *casts = pl.pallas_call(
        functools.partial(_matmul_kernel, n_side=side.n),
        grid=grid,
        in_specs=[pl.BlockSpec((tm, k), lambda i, j: (i, 0)), _weight_spec(w, k, tn)] + side.in_specs,
        out_specs=[pl.BlockSpec((tm, tn), lambda i, j: (i, j))] + side.out_specs,
        out_shape=[jax.ShapeDtypeStruct((t, n), BF16)] + side.out_shapes,
        compiler_params=_params(("arbitrary", "arbitrary"), blocks, _nbytes((tm, tn), F32)),
        name="projection",
    )(a, w.array, *side.inputs())
    return out, side.results(casts)


def _gate_up_kernel(h_ref, wg_ref, wu_ref, *rest, n_side):
    o_ref = rest[n_side]
    _run_side_casts(rest[:n_side], rest[n_side + 1:])
    h = h_ref[...]
    for cols in _column_halves(o_ref.shape[1]):
        g = jnp.dot(h, wg_ref[:, cols], preferred_element_type=F32)
        u = jnp.dot(h, wu_ref[:, cols], preferred_element_type=F32)
        o_ref[:, cols] = (g * jax.nn.sigmoid(g) * u).astype(o_ref.dtype)


def _gate_up(h, wg, wu, side_weights=()):
    t, k = h.shape
    n = wg.shape[1]
    tm, tn = _tile(t, 1024, 8), _tile(n, 512, 128)
    grid = (t // tm, n // tn)
    side = _SideCasts(side_weights, grid)
    blocks = _nbytes((tm, k), BF16) + 2 * _nbytes((k, tn), BF16) + _nbytes((tm, tn), BF16) + side.block_bytes
    out, *casts = pl.pallas_call(
        functools.partial(_gate_up_kernel, n_side=side.n),
        grid=grid,
        in_specs=[pl.BlockSpec((tm, k), lambda i, j: (i, 0)), _weight_spec(wg, k, tn), _weight_spec(wu, k, tn)]
        + side.in_specs,
        out_specs=[pl.BlockSpec((tm, tn), lambda i, j: (i, j))] + side.out_specs,
        out_shape=[jax.ShapeDtypeStruct((t, n), BF16)] + side.out_shapes,
        compiler_params=_params(("arbitrary", "arbitrary"), blocks, 3 * _nbytes((tm, tn), F32)),
        name="ffn_gate_up",
    )(h, wg.array, wu.array, *side.inputs())
    return out, side.results(casts)


def _residual_kernel(*refs, n_parts, coef):
    a_refs, w_refs = refs[:n_parts], refs[n_parts:2 * n_parts]
    x_ref, gate_ref, o_ref = refs[2 * n_parts:]
    for cols in _column_halves(o_ref.shape[1]):
        acc = jnp.dot(a_refs[0][...], w_refs[0][:, cols], preferred_element_type=F32)
        for a_ref, w_ref in zip(a_refs[1:], w_refs[1:]):
            acc += jnp.dot(a_ref[...], w_ref[:, cols], preferred_element_type=F32)
        o_ref[:, cols] = x_ref[:, cols] + (coef * gate_ref[0, :, cols]) * acc


def _residual_matmul(parts, w, x, gate, mod_rows, coef):
    t, n = x.shape
    widths = [p.shape[1] for p in parts]
    assert len(set(widths)) == 1 and sum(widths) == w.shape[0]
    kp = widths[0]
    tm = _tile(mod_rows.span, 1024 if _nbytes((1024, sum(widths)), BF16) <= LHS_BLOCK_BUDGET else 512, 8)
    tn = _tile(n, 512, 128)
    blocks = (len(parts) * (_nbytes((tm, kp), BF16) + _nbytes((kp, tn), BF16))
              + 2 * _nbytes((tm, tn), F32) + _nbytes((1, tn), F32))
    in_specs = [pl.BlockSpec((tm, kp), lambda i, j: (i, 0)) for _ in parts]
    in_specs += [_weight_spec(w, kp, tn, k_block=p) for p in range(len(parts))]
    in_specs += [pl.BlockSpec((tm, tn), lambda i, j: (i, j)),
                 pl.BlockSpec((1, 1, tn), lambda i, j: (mod_rows.of_tile(i, tm), 0, j))]
    return pl.pallas_call(
        functools.partial(_residual_kernel, n_parts=len(parts), coef=coef),
        grid=(t // tm, n // tn),
        in_specs=in_specs,
        out_specs=pl.BlockSpec((tm, tn), lambda i, j: (i, j)),
        out_shape=jax.ShapeDtypeStruct((t, n), F32),
        compiler_params=_params(("parallel", "parallel"), blocks, _nbytes((tm, tn), F32)),
        name="residual_matmul",
    )(*parts, *([w.array] * len(parts)), x, gate)


def _gelu_tanh(x):
    return x * (0.5 * (1.0 + jnp.tanh(math.sqrt(2.0 / math.pi) * (x + 0.044715 * (x * x * x)))))


def _gmlp_kernel(uv_ref, ws_ref, bs_ref, vg_ref, o_ref, *, n_groups, n_chunks):
    for g in range(n_groups):
        u_cols = slice(g * HEAD_DIM, (g + 1) * HEAD_DIM)
        v_cols = slice((n_groups + g) * HEAD_DIM, (n_groups + g + 1) * HEAD_DIM)
        w = ws_ref[g]
        b = bs_ref[g]
        vg = vg_ref[:, u_cols]

        for c in range(n_chunks):
            rows = slice(c * CHUNK, (c + 1) * CHUNK)
            u = _gelu_tanh(uv_ref[rows, u_cols].astype(F32))
            v = _gelu_tanh(uv_ref[rows, v_cols].astype(F32))
            v = v - jnp.mean(v, axis=-1, keepdims=True)
            vn = (v * lax.rsqrt(jnp.mean(v * v, axis=-1, keepdims=True) + NORM_EPS)) * vg
            s = jnp.dot(w, vn.astype(BF16), preferred_element_type=F32) + b
            o_ref[rows, u_cols] = (u * s).astype(o_ref.dtype)


def _gmlp(proj, w_s, b_s, vnorm_g):
    t = proj.shape[0]
    n_groups = w_s.shape[0]
    gm = n_groups * HEAD_DIM
    tc = _tile(t, 512, CHUNK)
    blocks = (_nbytes((tc, 2 * gm), BF16) + _nbytes((tc, gm), BF16) + _nbytes(w_s.shape, BF16)
              + _nbytes((n_groups, CHUNK, 128), F32) + _nbytes((8, gm), F32))
    return pl.pallas_call(
        functools.partial(_gmlp_kernel, n_groups=n_groups, n_chunks=tc // CHUNK),
        grid=(t // tc,),
        in_specs=[pl.BlockSpec((tc, 2 * gm), lambda i: (i, 0)),
                  pl.BlockSpec(w_s.shape, lambda i: (0, 0, 0)),
                  pl.BlockSpec((n_groups, CHUNK, 1), lambda i: (0, 0, 0)),
                  pl.BlockSpec((1, gm), lambda i: (0, 0))],
        out_specs=pl.BlockSpec((tc, gm), lambda i: (i, 0)),
        out_shape=jax.ShapeDtypeStruct((t, gm), BF16),
        compiler_params=_params(("parallel",), blocks, 16 * _nbytes((CHUNK, HEAD_DIM), F32)),
        name="gmlp_chunk_gating",
    )(proj, w_s.astype(BF16), b_s.reshape(n_groups, CHUNK, 1), vnorm_g.reshape(1, gm))


def _swap_rope_halves(x):
    lane = lax.broadcasted_iota(jnp.int32, x.shape, x.ndim - 1)
    return jnp.where(lane % 64 < 32, pltpu.roll(x, HEAD_DIM - 32, x.ndim - 1), pltpu.roll(x, 32, x.ndim - 1))


def _rms_head(x, g):
    x = x.astype(F32)
    return (x * lax.rsqrt(jnp.mean(x * x, axis=-1, keepdims=True) + NORM_EPS)) * g


def _head_norm_kernel(x_ref, g_ref, *rest, rope, heads):
    o_ref = rest[-1]
    for h in range(heads):
        cols = slice(h * HEAD_DIM, (h + 1) * HEAD_DIM)
        y = _rms_head(x_ref[:, cols], g_ref[:, cols])
        if rope:
            cos_ref, sin_ref = rest[:2]
            y = y * cos_ref[...] + _swap_rope_halves(y) * sin_ref[...]
        o_ref[:, cols] = y.astype(o_ref.dtype)


def _head_norm(src, first_head, gains, rope_tables=None):
    t = src.shape[0]
    n_heads = gains.shape[1] // HEAD_DIM
    heads = HEAD_NORM_GROUP if first_head % HEAD_NORM_GROUP == 0 and n_heads % HEAD_NORM_GROUP == 0 else 1
    width = heads * HEAD_DIM
    seq = rope_tables[0].shape[0] if rope_tables is not None else t
    tm = _tile(seq, 1024, 16)
    per_seq = seq // tm
    in_specs = [pl.BlockSpec((tm, width), lambda i, h: (i, first_head // heads + h)),
                pl.BlockSpec((1, width), lambda i, h: (0, h))]
    args = [src, gains]
    if rope_tables is not None:
        in_specs += [pl.BlockSpec((tm, HEAD_DIM), lambda i, h: (i % per_seq, 0))] * 2
        args += list(rope_tables)
    blocks = 2 * _nbytes((tm, width), BF16) + 2 * _nbytes((tm, HEAD_DIM), F32)
    return pl.pallas_call(
        functools.partial(_head_norm_kernel, rope=rope_tables is not None, heads=heads),
        grid=(t // tm, n_heads // heads),
        in_specs=in_specs,
        out_specs=pl.BlockSpec((tm, width), lambda i, h: (i, h)),
        out_shape=jax.ShapeDtypeStruct((t, n_heads * HEAD_DIM), BF16),
        compiler_params=_params(("parallel", "parallel"), blocks, 6 * heads * _nbytes((tm, HEAD_DIM), F32)),
        name="head_norm_rope" if rope_tables is not None else "head_norm",
    )(*args)


def _rope_tables(seq):
    half = HEAD_DIM // 2
    t = jnp.arange(seq)
    rows = (t // GRID_W).astype(F32)
    cols = (t % GRID_W).astype(F32)
    inv = 1.0 / (ROPE_THETA ** (jnp.arange(0, half, 2, dtype=F32) / half))
    ang = jnp.concatenate([rows[:, None] * inv, rows[:, None] * inv, cols[:, None] * inv, cols[:, None] * inv], -1)
    sign = jnp.where(jnp.arange(HEAD_DIM) % half < half // 2, -1.0, 1.0).astype(F32)
    return jnp.cos(ang), jnp.sin(ang) * sign


def _nt_dot(a, b):
    return lax.dot_general(a, b, (((1,), (1,)), ((), ())), preferred_element_type=F32)


def _sink_column(sink_ref, first, group, rows_per_head):
    head = lax.broadcasted_iota(jnp.int32, (group * rows_per_head, 1), 0) // rows_per_head
    col = jnp.zeros((group * rows_per_head, 1), F32)
    for g in range(group):
        col = jnp.where(head == g, sink_ref[first + g], col)
    return col


def _swa_kernel(sink_ref, q_ref, k_ref, v_ref, kc_ref, vc_ref, o_ref, *, tq, seq, group):
    kv_head, q_tile = pl.program_id(1), pl.program_id(2)
    kc, vc = kc_ref[0], vc_ref[0]
    sink = _sink_column(sink_ref, kv_head * group, group, BLOCK)
    q_off = lax.broadcasted_iota(jnp.int32, (group * BLOCK, 1), 0) % BLOCK
    k_off = lax.broadcasted_iota(jnp.int32, (1, 3 * BLOCK), 1)

    def scores(j):
        base = q_tile * tq + j * BLOCK
        start = pl.multiple_of(jnp.clip(base - BLOCK, 0, seq - 3 * BLOCK), BLOCK)
        q_rows = pl.ds(pl.multiple_of(j * BLOCK, BLOCK), BLOCK)
        q = jnp.concatenate([q_ref[0, q_rows, g * HEAD_DIM:(g + 1) * HEAD_DIM] for g in range(group)], axis=0)
        s_w = _nt_dot(q, k_ref[0, pl.ds(start, 3 * BLOCK), :])
        return base, start, q_rows, s_w, _nt_dot(q, kc)

    def attend(base, start, q_rows, s_w, s_c):
        s_w = jnp.where(jnp.abs((start + k_off) - (base + q_off)) <= WINDOW, s_w, NEG_INF)
        m = jnp.maximum(jnp.maximum(jnp.max(s_w, -1, keepdims=True), jnp.max(s_c, -1, keepdims=True)), sink)
        p_w = jnp.exp(s_w - m)
        p_c = jnp.exp(s_c - m)
        den = jnp.sum(p_w, -1, keepdims=True) + jnp.sum(p_c, -1, keepdims=True) + jnp.exp(sink - m)
        o = (jnp.dot(p_w.astype(BF16), v_ref[0, pl.ds(start, 3 * BLOCK), :], preferred_element_type=F32)
             + jnp.dot(p_c.astype(BF16), vc, preferred_element_type=F32)) / den
        for g in range(group):
            o_ref[0, q_rows, g * HEAD_DIM:(g + 1) * HEAD_DIM] = o[g * BLOCK:(g + 1) * BLOCK].astype(o_ref.dtype)

    def block_group(i, carry):
        staged = [scores(i * SWA_BLOCK_UNROLL + u) for u in range(SWA_BLOCK_UNROLL)]
        for block_state in staged:
            attend(*block_state)
        return carry

    assert (tq // BLOCK) % SWA_BLOCK_UNROLL == 0
    lax.fori_loop(0, tq // BLOCK // SWA_BLOCK_UNROLL, block_group, 0)


def _swa(qk, proj, kc, projc, sink, *, n_heads, v_col, batch, seq):
    group = n_heads // SWA_KV_HEADS
    lc = kc.shape[0] // batch
    gw = group * HEAD_DIM
    tq = _tile(seq, 1024, BLOCK)
    qk3 = qk.reshape(batch, seq, qk.shape[1])
    proj3 = proj.reshape(batch, seq, proj.shape[1])
    kc3 = kc.reshape(batch, lc, kc.shape[1])
    projc3 = projc.reshape(batch, lc, projc.shape[1])
    blocks = (2 * _nbytes((tq, gw), BF16) + 2 * _nbytes((seq, HEAD_DIM), BF16) + 2 * _nbytes((lc, HEAD_DIM), BF16))
    temps = 6 * _nbytes((group * BLOCK, 3 * BLOCK + lc), F32)
    out = pl.pallas_call(
        functools.partial(_swa_kernel, tq=tq, seq=seq, group=group),
        grid=(batch, SWA_KV_HEADS, seq // tq),
        in_specs=[pl.BlockSpec(memory_space=pltpu.SMEM),
                  pl.BlockSpec((1, tq, gw), lambda b, h, i: (b, i, h)),
                  pl.BlockSpec((1, seq, HEAD_DIM), lambda b, h, i: (b, 0, n_heads + h)),
                  pl.BlockSpec((1, seq, HEAD_DIM), lambda b, h, i: (b, 0, v_col + h)),
                  pl.BlockSpec((1, lc, HEAD_DIM), lambda b, h, i: (b, 0, h)),
                  pl.BlockSpec((1, lc, HEAD_DIM), lambda b, h, i: (b, 0, v_col + h))],
        out_specs=pl.BlockSpec((1, tq, gw), lambda b, h, i: (b, i, h)),
        out_shape=jax.ShapeDtypeStruct((batch, seq, n_heads * HEAD_DIM), BF16),
        compiler_params=_params(("parallel", "parallel", "parallel"), blocks, temps),
        name="window_attention",
    )(sink, qk3, qk3, proj3, kc3, projc3)
    return out.reshape(batch * seq, n_heads * HEAD_DIM)


def _ctx_attn_kernel(*refs, group, lc, use_sink):
    if use_sink:
        sink_ref, q_ref, k_ref, v_ref, o_ref = refs
    else:
        q_ref, k_ref, v_ref, o_ref = refs
    q = jnp.concatenate([q_ref[0, :, g * HEAD_DIM:(g + 1) * HEAD_DIM] for g in range(group)], axis=0)
    s = _nt_dot(q, k_ref[0])
    m = jnp.max(s, -1, keepdims=True)
    if use_sink:
        sink = _sink_column(sink_ref, pl.program_id(1) * group, group, lc)
        m = jnp.maximum(m, sink)
    p = jnp.exp(s - m)
    den = jnp.sum(p, -1, keepdims=True)
    if use_sink:
        den = den + jnp.exp(sink - m)
    o = jnp.dot(p.astype(BF16), v_ref[0], preferred_element_type=F32) / den
    for g in range(group):
        o_ref[0, :, g * HEAD_DIM:(g + 1) * HEAD_DIM] = o[g * lc:(g + 1) * lc].astype(o_ref.dtype)


def _ctx_attn(qc, kc, projc, sink, *, n_heads, n_kv, v_col, batch):
    group = n_heads // n_kv
    lc = qc.shape[0] // batch
    gw = group * HEAD_DIM
    qc3 = qc.reshape(batch, lc, qc.shape[1])
    kc3 = kc.reshape(batch, lc, kc.shape[1])
    projc3 = projc.reshape(batch, lc, projc.shape[1])
    use_sink = sink is not None
    in_specs = [pl.BlockSpec((1, lc, gw), lambda b, h: (b, 0, h)),
                pl.BlockSpec((1, lc, HEAD_DIM), lambda b, h: (b, 0, h)),
                pl.BlockSpec((1, lc, HEAD_DIM), lambda b, h: (b, 0, v_col + h))]
    args = [qc3, kc3, projc3]
    if use_sink:
        in_specs = [pl.BlockSpec(memory_space=pltpu.SMEM)] + in_specs
        args = [sink] + args
    blocks = 2 * _nbytes((lc, gw), BF16) + 2 * _nbytes((lc, HEAD_DIM), BF16)
    out = pl.pallas_call(
        functools.partial(_ctx_attn_kernel, group=group, lc=lc, use_sink=use_sink),
        grid=(batch, n_kv),
        in_specs=in_specs,
        out_specs=pl.BlockSpec((1, lc, gw), lambda b, h: (b, 0, h)),
        out_shape=jax.ShapeDtypeStruct((batch, lc, n_heads * HEAD_DIM), BF16),
        compiler_params=_params(("parallel", "parallel"), blocks, 6 * _nbytes((group * lc, lc), F32)),
        name="context_attention",
    )(*args)
    return out.reshape(batch * lc, n_heads * HEAD_DIM)


def _nat_bias_kernel(rpb_ref, o_ref):
    h = pl.program_id(0)
    n_dc = 2 * NA_KW - 1
    w = lax.broadcasted_iota(jnp.int32, (GRID_W, 2 * GRID_W), 0)
    lane = lax.broadcasted_iota(jnp.int32, (GRID_W, 2 * GRID_W), 1)
    second = lane >= GRID_W
    kc = lane % GRID_W
    c0 = jnp.clip(w - NA_KW // 2, 0, GRID_W - NA_KW)
    dc = jnp.where((kc >= c0) & (kc < c0 + NA_KW), kc - w + (NA_KW - 1), -1)

    def pair(dr, carry):
        base = (h * (2 * NA_KH - 1) + dr) * n_dc
        tile = jnp.full((GRID_W, 2 * GRID_W), NEG_INF, F32)
        for j in range(n_dc):
            tile = jnp.where(dc == j, jnp.where(second, rpb_ref[base + n_dc + j], rpb_ref[base + j]), tile)
        o_ref[0, dr] = tile
        return carry

    lax.fori_loop(0, 2 * NA_KH - 2, pair, 0)


def _nat_bias(rpb):
    n_heads = rpb.shape[0]
    assert rpb.shape[1:] == (2 * NA_KH - 1, 2 * NA_KW - 1)
    shape = (n_heads, 2 * NA_KH - 2, GRID_W, 2 * GRID_W)
    return pl.pallas_call(
        _nat_bias_kernel,
        grid=(n_heads,),
        in_specs=[pl.BlockSpec(memory_space=pltpu.SMEM)],
        out_specs=pl.BlockSpec((1,) + shape[1:], lambda h: (h, 0, 0, 0)),
        out_shape=jax.ShapeDtypeStruct(shape, F32),
        compiler_params=_params(("parallel",), _nbytes(shape[1:], F32)),
        name="nat_bias_tiles",
    )(rpb.reshape(-1))


def _nat_kernel(q_ref, k_ref, v_ref, kc_ref, vc_ref, gq_ref, gk_ref, bias_ref, o_ref, kn_ref, *, rows, norm_rows,
                heads, group_rows):
    n_pairs = NA_KH // 2
    gq = gq_ref[...] * ATTN_SCALE
    gk = gk_ref[...]
    n_ctx = kc_ref.shape[1] // HEAD_DIM

    for head in range(heads):
        cols = slice(head * HEAD_DIM, (head + 1) * HEAD_DIM)

        def norm_keys(i, carry, cols=cols):
            chunk = pl.ds(pl.multiple_of(i * norm_rows, norm_rows), norm_rows)
            kn_ref[chunk, :] = _rms_head(k_ref[0, chunk, cols], gk).astype(BF16)
            return carry

        lax.fori_loop(0, rows * GRID_W // norm_rows, norm_keys, 0, unroll=2)
        kc = _rms_head(kc_ref[0, :, cols], gk).astype(BF16)
        vc = vc_ref[0, :, cols]

        def window(r):
            r0 = jnp.clip(r - NA_KH // 2, 0, rows - NA_KH)
            q_rows = pl.ds(pl.multiple_of(r * GRID_W, GRID_W), GRID_W)
            k_rows = pl.ds(pl.multiple_of(r0 * GRID_W, GRID_W), NA_KH * GRID_W)
            return q_rows, k_rows, r0 - r + (NA_KH - 1)

        def logits(dr0, s_nb, s_c, head=head):
            s = [s_nb[:, 2 * GRID_W * j:2 * GRID_W * (j + 1)] + bias_ref[head, dr0 + 2 * j] for j in range(n_pairs)]
            s += [s_c[:, HEAD_DIM * j:HEAD_DIM * (j + 1)] for j in range(n_ctx)]
            return s, jnp.max(functools.reduce(jnp.maximum, s), -1, keepdims=True)

        def row_group(i, carry, cols=cols, kc=kc, vc=vc, window=window, logits=logits):
            wins = [window(i * group_rows + u) for u in range(group_rows)]
            q = [_rms_head(q_ref[0, q_rows, cols], gq).astype(BF16) for q_rows, _, _ in wins]
            s_c = _nt_dot(jnp.concatenate(q, axis=0), kc)
            s_nb = [_nt_dot(qu, kn_ref[k_rows, :]) for qu, (_, k_rows, _) in zip(q, wins)]
            staged = [logits(dr0, s_nb[u], s_c[u * GRID_W:(u + 1) * GRID_W]) for u, (_, _, dr0) in enumerate(wins)]
            p = [[jnp.exp(sj - m) for sj in s] for s, m in staged]
            den = [jnp.sum(functools.reduce(jnp.add, pu), -1, keepdims=True) for pu in p]
            o_nb = [jnp.dot(jnp.concatenate(pu[:n_pairs], axis=-1).astype(BF16), v_ref[0, k_rows, cols],
                            preferred_element_type=F32) for pu, (_, k_rows, _) in zip(p, wins)]
            p_c = [jnp.concatenate(pu[n_pairs:], axis=-1).astype(BF16) for pu in p]
            o_c = jnp.dot(jnp.concatenate(p_c, axis=0), vc, preferred_element_type=F32)
            for u, (q_rows, _, _) in enumerate(wins):
                o = (o_nb[u] + o_c[u * GRID_W:(u + 1) * GRID_W]) / den[u]
                o_ref[0, q_rows, cols] = o.astype(o_ref.dtype)
            return carry

        lax.fori_loop(0, rows // group_rows, row_group, 0)


def _nat(proj, projc, gq, gk, bias, *, n_heads, batch, seq):
    rows = seq // GRID_W
    assert seq % GRID_W == 0 and rows >= NA_KH
    group_rows = math.gcd(rows, NAT_ROW_UNROLL)
    lc = projc.shape[0] // batch
    assert lc % HEAD_DIM == 0
    heads = NAT_HEADS_PER_STEP if n_heads % NAT_HEADS_PER_STEP == 0 else 1
    width = heads * HEAD_DIM
    norm_rows = _tile(seq, 512, 16)
    proj3 = proj.reshape(batch, seq, proj.shape[1])
    projc3 = projc.reshape(batch, lc, projc.shape[1])

    def head_block(n, first_head):
        return pl.BlockSpec((1, n, width), lambda b, h: (b, 0, first_head // heads + h))

    gain_block = pl.BlockSpec((1, HEAD_DIM), lambda b, h: (0, 0))
    bias_block = (heads,) + bias.shape[1:]
    blocks = 4 * _nbytes((seq, width), BF16) + 2 * _nbytes((lc, width), BF16) + _nbytes(bias_block, F32)
    temps = (_nbytes((seq, HEAD_DIM), BF16)
             + 4 * group_rows * _nbytes((GRID_W, NA_KH * GRID_W + lc), F32) + 4 * _nbytes((norm_rows, HEAD_DIM), F32))
    out = pl.pallas_call(
        functools.partial(_nat_kernel, rows=rows, norm_rows=norm_rows, heads=heads, group_rows=group_rows),
        grid=(batch, n_heads // heads),
        in_specs=[head_block(seq, 0), head_block(seq, n_heads), head_block(seq, 2 * n_heads),
                  head_block(lc, n_heads), head_block(lc, 2 * n_heads),
                  gain_block, gain_block,
                  pl.BlockSpec(bias_block, lambda b, h: (h, 0, 0, 0))],
        out_specs=head_block(seq, 0),
        out_shape=jax.ShapeDtypeStruct((batch, seq, n_heads * HEAD_DIM), BF16),
        scratch_shapes=[pltpu.VMEM((seq, HEAD_DIM), BF16)],
        compiler_params=_params(("parallel", "parallel"), blocks, temps),
        name="neighbourhood_attention",
    )(proj3, proj3, proj3, projc3, projc3, gq.reshape(1, HEAD_DIM), gk.reshape(1, HEAD_DIM), bias)
    return out.reshape(batch * seq, n_heads * HEAD_DIM)


def _gains(*groups):
    return jnp.concatenate([jnp.tile(g.astype(F32), n) for g, n in groups]).reshape(1, -1)


def kernel(x, c, ctx, c_ctx, w_ada, b_ada, norm_g, ffn_w_gate, ffn_w_up, ffn_w_down, mix_ab_w_in, mix_ab_w_out,
           gm_vnorm_g, gm_w_s, gm_b_s, swa_qnorm_g, swa_knorm_g, swa_sink, nat_w_qkv, nat_w_out, nat_qnorm_g,
           nat_knorm_g, nat_rpb):
    batch, seq, d = x.shape
    lc = ctx.shape[1]
    depth = w_ada.shape[0]
    ctx_row = batch
    assert batch + 1 <= MOD_ROWS

    cond = jnp.zeros((MOD_ROWS, d), F32).at[:batch].set(c).at[ctx_row].set(c_ctx)
    mod = _ada(cond, w_ada, b_ada).reshape(depth, MOD_ROWS, N_MOD, 1, d)

    lat_row = _ModRows(0, seq)
    ctx_row_of = _ModRows(ctx_row, batch * lc)

    def mod_vec(layer, sub, which):
        return mod[layer, :, 3 * sub + which]

    def ffn_in(xs, row, layer, sub):
        return _norm_mod(xs, norm_g[layer, sub], mod_vec(layer, sub, 0), mod_vec(layer, sub, 1), row)

    def ffn_out(a, wd, xs, row, layer, sub):
        return _residual_matmul([a], wd, xs, mod_vec(layer, sub, 2), row, HALF_STEP)

    def ffn_ctx(cs, layer, sub, wg, wu, wd):
        a, _ = _gate_up(ffn_in(cs, ctx_row_of, layer, sub), wg, wu)
        return ffn_out(a, wd, cs, ctx_row_of, layer, sub)

    def ffn_f32(layer, idx):
        return (_Weight(w, (layer, idx)) for w in (ffn_w_gate, ffn_w_up, ffn_w_down))

    xs = x.reshape(batch * seq, d)
    cs = ctx.reshape(batch * lc, d)
    rope = _rope_tables(seq)

    wg0_f32, wu0_f32, _ = ffn_f32(0, 0)
    wg, wu = (_Weight(w.array[w.lead].astype(BF16)) for w in (wg0_f32, wu0_f32))

    for layer in range(depth):
        last = layer == depth - 1
        j = layer // 2
        even = layer % 2 == 0
        mix_in_f32 = _Weight(mix_ab_w_in if even else nat_w_qkv, (j,))
        mix_out_f32 = _Weight(mix_ab_w_out if even else nat_w_out, (j,))
        _, _, wd_f32 = ffn_f32(layer, 0)
        a, (wd, mix_in, mix_out) = _gate_up(ffn_in(xs, lat_row, layer, 0), wg, wu, [wd_f32, mix_in_f32, mix_out_f32])
        xs = ffn_out(a, wd, xs, lat_row, layer, 0)
        cs = ffn_ctx(cs, layer, 0, wg, wu, wd)
        h = _norm_mod(xs, norm_g[layer, 1], mod_vec(layer, 1, 0), mod_vec(layer, 1, 1), lat_row)
        hc = _norm_mod(cs, norm_g[layer, 1], mod_vec(layer, 1, 0), mod_vec(layer, 1, 1), ctx_row_of)
        gate = mod_vec(layer, 1, 2)
        wg1_f32, wu1_f32, wd1_f32 = ffn_f32(layer, 1)
        proj, (wg, wu) = _matmul(h, mix_in, [wg1_f32, wu1_f32])
        projc, _ = _matmul(hc, mix_in)
        if even:
            n_groups = gm_w_s.shape[1]
            n_heads = swa_sink.shape[1]
            assert n_groups == n_heads
            q_col = 2 * n_groups
            v_col = q_col + n_heads + SWA_KV_HEADS
            qk_gain = _gains((swa_qnorm_g[j] * ATTN_SCALE, n_heads), (swa_knorm_g[j], SWA_KV_HEADS))
            qk = _head_norm(proj, q_col, qk_gain, rope)
            qkc = _head_norm(projc, q_col, qk_gain)
            kc = qkc[:, n_heads * HEAD_DIM:]
            y_g = _gmlp(proj, gm_w_s[j], gm_b_s[j], gm_vnorm_g[j])
            y_a = _swa(qk, proj, kc, projc, swa_sink[j], n_heads=n_heads, v_col=v_col, batch=batch, seq=seq)
            xs = _residual_matmul([y_g, y_a], mix_out, xs, gate, lat_row, 1.0)
            if not last:
                yc_g = _gmlp(projc, gm_w_s[j], gm_b_s[j], gm_vnorm_g[j])
                yc_a = _ctx_attn(qkc, kc, projc, swa_sink[j], n_heads=n_heads, n_kv=SWA_KV_HEADS, v_col=v_col,
                                 batch=batch)
                cs = _residual_matmul([yc_g, yc_a], mix_out, cs, gate, ctx_row_of, 1.0)
        else:
            n_heads = nat_rpb.shape[1]
            y = _nat(proj, projc, nat_qnorm_g[j], nat_knorm_g[j], _nat_bias(nat_rpb[j]), n_heads=n_heads,
                     batch=batch, seq=seq)
            xs = _residual_matmul([y], mix_out, xs, gate, lat_row, 1.0)
            if not last:
                qk_gain = _gains((nat_qnorm_g[j] * ATTN_SCALE, n_heads), (nat_knorm_g[j], n_heads))
                qkc = _head_norm(projc, 0, qk_gain)
                yc = _ctx_attn(qkc, qkc[:, n_heads * HEAD_DIM:], projc, None, n_heads=n_heads, n_kv=n_heads,
                               v_col=2 * n_heads, batch=batch)
                cs = _residual_matmul([yc], mix_out, cs, gate, ctx_row_of, 1.0)
        next_f32 = [] if last else list(ffn_f32(layer + 1, 0))[:2]
        a, (wd1, *next_w) = _gate_up(ffn_in(xs, lat_row, layer, 2), wg, wu, [wd1_f32] + next_f32)
        xs = ffn_out(a, wd1, xs, lat_row, layer, 2)
        if not last:
            cs = ffn_ctx(cs, layer, 2, wg, wu, wd1)
            wg, wu = next_w
    return xs.reshape(batch, seq, d)
```

```python
import functools
import math
from typing import NamedTuple

import jax
import jax.numpy as jnp
from jax import lax
from jax.experimental import pallas as pl
from jax.experimental.pallas import tpu as pltpu

F32 = jnp.float32
BF16 = jnp.bfloat16

HEAD_DIM = 128
GRID_W = 64
CHUNK = 128
WINDOW = 128
BLOCK = 128
SWA_KV_HEADS = 4
NA_KH = 8
NA_KW = 16
NAT_ROW_UNROLL = 32
NAT_HEADS_PER_STEP = 1
SWA_BLOCK_UNROLL = 4
HEAD_NORM_GROUP = 4
N_MOD = 9
HALF_STEP = 0.5
ROPE_THETA = 10000.0
NORM_EPS = 1e-6
NEG_INF = -1e30
ATTN_SCALE = HEAD_DIM ** -0.5

V7X_VMEM_BYTES = 64 * 1024 * 1024
VMEM_CEILING = V7X_VMEM_BYTES - 8 * 1024 * 1024
MOD_ROWS = 8
BF16_SUBLANES = 16
STREAM_SPLITS = 4
MXU_WIDTH = 256
NORM_ROW_GROUP = BF16_SUBLANES
NORM_GROUP_UNROLL = 4
LHS_BLOCK_BUDGET = 8 * 1024 * 1024


def _vmem_limit(pipelined_bytes, temp_bytes=0):
    need = int((2 * pipelined_bytes + temp_bytes) * 1.25) + (2 << 20)
    return min(max(need, 16 << 20), VMEM_CEILING)


def _fits_vmem(pipelined_bytes, temp_bytes=0):
    return 2 * pipelined_bytes + temp_bytes <= VMEM_CEILING * 9 // 10


def _params(semantics, pipelined_bytes, temp_bytes=0):
    return pltpu.CompilerParams(dimension_semantics=semantics,
                                vmem_limit_bytes=_vmem_limit(pipelined_bytes, temp_bytes))


def _tile(n, pref, quantum):
    if n <= pref:
        return n
    t = (pref // quantum) * quantum
    while t > quantum and n % t:
        t -= quantum
    assert n % t == 0, (n, pref, quantum)
    return t


def _nbytes(shape, dtype):
    return math.prod(shape) * jnp.dtype(dtype).itemsize


def _ada_kernel(c_ref, *refs):
    w_refs, (b_ref, o_ref) = refs[:-2], refs[-2:]
    c = c_ref[...]
    s = (c * jax.nn.sigmoid(c)).astype(BF16)
    k = w_refs[0].shape[1]
    acc = b_ref[0]
    for part, w_ref in enumerate(w_refs):
        acc = acc + jnp.dot(s[:, part * k:(part + 1) * k], w_ref[0].astype(BF16), preferred_element_type=F32)
    o_ref[0] = acc


def _ada(cond, w_ada, b_ada):
    depth, d, n = w_ada.shape
    tn = _tile(n, 512, 128)
    parts = STREAM_SPLITS if d % (STREAM_SPLITS * HEAD_DIM) == 0 else 1
    k = d // parts
    blocks = _nbytes((MOD_ROWS, d), F32) + _nbytes((d, tn), F32) + 2 * _nbytes((MOD_ROWS, tn), F32)
    w_specs = [pl.BlockSpec((1, k, tn), functools.partial(lambda l, j, p: (l, p, j), p=p)) for p in range(parts)]
    return pl.pallas_call(
        _ada_kernel,
        grid=(depth, n // tn),
        in_specs=[pl.BlockSpec((MOD_ROWS, d), lambda l, j: (0, 0))] + w_specs
        + [pl.BlockSpec((1, 1, tn), lambda l, j: (l, 0, j))],
        out_specs=pl.BlockSpec((1, MOD_ROWS, tn), lambda l, j: (l, 0, j)),
        out_shape=jax.ShapeDtypeStruct((depth, MOD_ROWS, n), F32),
        compiler_params=_params(("parallel", "parallel"), blocks, _nbytes((d, tn), BF16)),
        name="ada_modulation",
    )(cond, *([w_ada] * parts), b_ada.reshape(depth, 1, n))


def _norm_mod_kernel(*refs):
    x_refs, (g_ref, shift_ref, scale_ref, o_ref) = refs[:-4], refs[-4:]
    gain = g_ref[...] * (1.0 + scale_ref[0])
    shift = shift_ref[0]
    part_rows = x_refs[0].shape[0]

    for part, x_ref in enumerate(x_refs):
        def row_group(i, carry, x_ref=x_ref, part=part):
            start = pl.multiple_of(i * NORM_ROW_GROUP, NORM_ROW_GROUP)
            x = x_ref[pl.ds(start, NORM_ROW_GROUP), :]
            y = x * lax.rsqrt(jnp.mean(x * x, axis=-1, keepdims=True) + NORM_EPS)
            o_ref[pl.ds(part * part_rows + start, NORM_ROW_GROUP), :] = (y * gain + shift).astype(o_ref.dtype)
            return carry

        lax.fori_loop(0, part_rows // NORM_ROW_GROUP, row_group, 0, unroll=NORM_GROUP_UNROLL)


class _ModRows:
    def __init__(self, first, span):
        self.first, self.span = first, span

    def of_tile(self, i, tm):
        assert self.span % tm == 0
        return self.first + (i * tm) // self.span


def _norm_mod(x, g, shift, scale, mod_rows):
    t, d = x.shape
    tm = _tile(mod_rows.span, 256, NORM_ROW_GROUP)
    parts = STREAM_SPLITS if tm % (STREAM_SPLITS * NORM_ROW_GROUP * NORM_GROUP_UNROLL) == 0 else 1
    mod_spec = pl.BlockSpec((1, 1, d), lambda i: (mod_rows.of_tile(i, tm), 0, 0))
    blocks = _nbytes((tm, d), F32) + _nbytes((tm, d), BF16) + 3 * _nbytes((1, d), F32)
    x_specs = [pl.BlockSpec((tm // parts, d), functools.partial(lambda i, p: (i * parts + p, 0), p=p))
               for p in range(parts)]
    return pl.pallas_call(
        _norm_mod_kernel,
        grid=(t // tm,),
        in_specs=x_specs + [pl.BlockSpec((1, d), lambda i: (0, 0)), mod_spec, mod_spec],
        out_specs=pl.BlockSpec((tm, d), lambda i: (i, 0)),
        out_shape=jax.ShapeDtypeStruct((t, d), BF16),
        compiler_params=_params(("parallel",), blocks, 2 * _nbytes((tm, d), F32)),
        name="norm_modulate",
    )(*([x] * parts), g.reshape(1, d), shift, scale)


class _Weight(NamedTuple):
    array: jax.Array
    lead: tuple = ()

    @property
    def shape(self):
        return self.array.shape[-2:]


def _weight_spec(w, k, tn, k_block=0):
    return pl.BlockSpec((None,) * len(w.lead) + (k, tn), lambda i, j: w.lead + (k_block, j))


def _cast_rows(rows, n_steps):
    for r in range(BF16_SUBLANES, rows + 1, BF16_SUBLANES):
        if rows % r == 0 and rows // r <= n_steps:
            return r
    return None


class _SideCasts:
    def __init__(self, weights, grid):
        n_steps, n_inner = grid[0] * grid[1], grid[1]
        self.weights = list(weights)
        self.rows = [_cast_rows(w.shape[0], n_steps) for w in self.weights]
        self.carried = [w for w, r in zip(self.weights, self.rows) if r is not None]
        self.in_specs, self.out_specs, self.out_shapes, self.block_bytes = [], [], [], 0
        for w, r in zip(self.weights, self.rows):
            if r is None:
                continue
            rows, cols = w.shape
            block = functools.partial(lambda i, j, last: jnp.minimum(i * n_inner + j, last), last=rows // r - 1)
            self.in_specs.append(pl.BlockSpec((None,) * len(w.lead) + (r, cols),
                                              functools.partial(lambda i, j, w, block: w.lead + (block(i, j), 0),
                                                                w=w, block=block)))
            self.out_specs.append(pl.BlockSpec((r, cols), functools.partial(lambda i, j, block: (block(i, j), 0),
                                                                            block=block)))
            self.out_shapes.append(jax.ShapeDtypeStruct((rows, cols), BF16))
            self.block_bytes += _nbytes((r, cols), F32) + _nbytes((r, cols), BF16)

    @property
    def n(self):
        return len(self.carried)

    def inputs(self):
        return [w.array for w in self.carried]

    def results(self, outs):
        outs = iter(outs)
        return [_Weight(next(outs)) if r is not None else _Weight(w.array[w.lead].astype(BF16))
                for w, r in zip(self.weights, self.rows)]


def _column_halves(n):
    if n % (2 * MXU_WIDTH):
        return (slice(0, n),)
    return (slice(0, n // 2), slice(n // 2, n))


def _run_side_casts(in_refs, out_refs):
    for src, dst in zip(in_refs, out_refs):
        dst[...] = src[...].astype(dst.dtype)


def _matmul_kernel(a_ref, w_ref, o_ref):
    for cols in _column_halves(o_ref.shape[1]):
        o_ref[:, cols] = jnp.dot(a_ref[...], w_ref[:, cols], preferred_element_type=F32).astype(o_ref.dtype)


def _matmul(a, w):
    t, k = a.shape
    n = w.shape[1]
    tm = _tile(t, 1024, 8)
    for tn in (_tile(n, 1024, 128), _tile(n, 512, 128)):
        blocks = _nbytes((tm, k), BF16) + _nbytes((k, tn), BF16) + _nbytes((tm, tn), BF16)
        if _fits_vmem(blocks, _nbytes((tm, tn), F32)):
            break
    return pl.pallas_call(
        _matmul_kernel,
        grid=(t // tm, n // tn),
        in_specs=[pl.BlockSpec((tm, k), lambda i, j: (i, 0)), _weight_spec(w, k, tn)],
        out_specs=pl.BlockSpec((tm, tn), lambda i, j: (i, j)),
        out_shape=jax.ShapeDtypeStruct((t, n), BF16),
        compiler_params=_params(("parallel", "parallel"), blocks, _nbytes((tm, tn), F32)),
        name="projection",
    )(a, w.array)


def _gate_up_kernel(h_ref, wg_ref, wu_ref, *rest, n_side):
    o_ref = rest[n_side]
    _run_side_casts(rest[:n_side], rest[n_side + 1:])
    h = h_ref[...]
    for cols in _column_halves(o_ref.shape[1]):
        g = jnp.dot(h, wg_ref[:, cols], preferred_element_type=F32)
        u = jnp.dot(h, wu_ref[:, cols], preferred_element_type=F32)
        o_ref[:, cols] = (g * jax.nn.sigmoid(g) * u).astype(o_ref.dtype)


def _gate_up(h, wg, wu, side_weights=()):
    t, k = h.shape
    n = wg.shape[1]
    tm, tn = _tile(t, 1024, 8), _tile(n, 512, 128)
    grid = (t // tm, n // tn)
    side = _SideCasts(side_weights, grid)
    blocks = _nbytes((tm, k), BF16) + 2 * _nbytes((k, tn), BF16) + _nbytes((tm, tn), BF16) + side.block_bytes
    out, *casts = pl.pallas_call(
        functools.partial(_gate_up_kernel, n_side=side.n),
        grid=grid,
        in_specs=[pl.BlockSpec((tm, k), lambda i, j: (i, 0)), _weight_spec(wg, k, tn), _weight_spec(wu, k, tn)]
        + side.in_specs,
        out_specs=[pl.BlockSpec((tm, tn), lambda i, j: (i, j))] + side.out_specs,
        out_shape=[jax.ShapeDtypeStruct((t, n), BF16)] + side.out_shapes,
        compiler_params=_params(("arbitrary", "arbitrary"), blocks, 3 * _nbytes((tm, tn), F32)),
        name="ffn_gate_up",
    )(h, wg.array, wu.array, *side.inputs())
    return out, side.results(casts)


def _residual_kernel(*refs, n_parts, coef):
    a_refs, w_refs = refs[:n_parts], refs[n_parts:2 * n_parts]
    x_ref, gate_ref, o_ref = refs[2 * n_parts:]
    for cols in _column_halves(o_ref.shape[1]):
        acc = jnp.dot(a_refs[0][...], w_refs[0][:, cols], preferred_element_type=F32)
        for a_ref, w_ref in zip(a_refs[1:], w_refs[1:]):
            acc += jnp.dot(a_ref[...], w_ref[:, cols], preferred_element_type=F32)
        o_ref[:, cols] = x_ref[:, cols] + (coef * gate_ref[0, :, cols]) * acc


def _residual_matmul(parts, w, x, gate, mod_rows, coef):
    t, n = x.shape
    widths = [p.shape[1] for p in parts]
    assert len(set(widths)) == 1 and sum(widths) == w.shape[0]
    kp = widths[0]
    tm = _tile(mod_rows.span, 1024 if _nbytes((1024, sum(widths)), BF16) <= LHS_BLOCK_BUDGET else 512, 8)
    tn = _tile(n, 512, 128)
    blocks = (len(parts) * (_nbytes((tm, kp), BF16) + _nbytes((kp, tn), BF16))
              + 2 * _nbytes((tm, tn), F32) + _nbytes((1, tn), F32))
    in_specs = [pl.BlockSpec((tm, kp), lambda i, j: (i, 0)) for _ in parts]
    in_specs += [_weight_spec(w, kp, tn, k_block=p) for p in range(len(parts))]
    in_specs += [pl.BlockSpec((tm, tn), lambda i, j: (i, j)),
                 pl.BlockSpec((1, 1, tn), lambda i, j: (mod_rows.of_tile(i, tm), 0, j))]
    return pl.pallas_call(
        functools.partial(_residual_kernel, n_parts=len(parts), coef=coef),
        grid=(t // tm, n // tn),
        in_specs=in_specs,
        out_specs=pl.BlockSpec((tm, tn), lambda i, j: (i, j)),
        out_shape=jax.ShapeDtypeStruct((t, n), F32),
        compiler_params=_params(("parallel", "parallel"), blocks, _nbytes((tm, tn), F32)),
        name="residual_matmul",
    )(*parts, *([w.array] * len(parts)), x, gate)


def _gelu_tanh(x):
    return x * (0.5 * (1.0 + jnp.tanh(math.sqrt(2.0 / math.pi) * (x + 0.044715 * (x * x * x)))))


def _gmlp_kernel(uv_ref, ws_ref, bs_ref, vg_ref, o_ref, *, n_groups, n_chunks):
    for g in range(n_groups):
        u_cols = slice(g * HEAD_DIM, (g + 1) * HEAD_DIM)
        v_cols = slice((n_groups + g) * HEAD_DIM, (n_groups + g + 1) * HEAD_DIM)
        w = ws_ref[g]
        b = bs_ref[g]
        vg = vg_ref[:, u_cols]

        for c in range(n_chunks):
            rows = slice(c * CHUNK, (c + 1) * CHUNK)
            u = _gelu_tanh(uv_ref[rows, u_cols].astype(F32))
            v = _gelu_tanh(uv_ref[rows, v_cols].astype(F32))
            v = v - jnp.mean(v, axis=-1, keepdims=True)
            vn = (v * lax.rsqrt(jnp.mean(v * v, axis=-1, keepdims=True) + NORM_EPS)) * vg
            s = jnp.dot(w, vn.astype(BF16), preferred_element_type=F32) + b
            o_ref[rows, u_cols] = (u * s).astype(o_ref.dtype)


def _gmlp(proj, w_s, b_s, vnorm_g):
    t = proj.shape[0]
    n_groups = w_s.shape[0]
    gm = n_groups * HEAD_DIM
    tc = _tile(t, 512, CHUNK)
    blocks = (_nbytes((tc, 2 * gm), BF16) + _nbytes((tc, gm), BF16) + _nbytes(w_s.shape, BF16)
              + _nbytes((n_groups, CHUNK, 128), F32) + _nbytes((8, gm), F32))
    return pl.pallas_call(
        functools.partial(_gmlp_kernel, n_groups=n_groups, n_chunks=tc // CHUNK),
        grid=(t // tc,),
        in_specs=[pl.BlockSpec((tc, 2 * gm), lambda i: (i, 0)),
                  pl.BlockSpec(w_s.shape, lambda i: (0, 0, 0)),
                  pl.BlockSpec((n_groups, CHUNK, 1), lambda i: (0, 0, 0)),
                  pl.BlockSpec((1, gm), lambda i: (0, 0))],
        out_specs=pl.BlockSpec((tc, gm), lambda i: (i, 0)),
        out_shape=jax.ShapeDtypeStruct((t, gm), BF16),
        compiler_params=_params(("parallel",), blocks, 16 * _nbytes((CHUNK, HEAD_DIM), F32)),
        name="gmlp_chunk_gating",
    )(proj, w_s.astype(BF16), b_s.reshape(n_groups, CHUNK, 1), vnorm_g.reshape(1, gm))


def _swap_rope_halves(x):
    lane = lax.broadcasted_iota(jnp.int32, x.shape, x.ndim - 1)
    return jnp.where(lane % 64 < 32, pltpu.roll(x, HEAD_DIM - 32, x.ndim - 1), pltpu.roll(x, 32, x.ndim - 1))


def _rms_head(x, g):
    x = x.astype(F32)
    return (x * lax.rsqrt(jnp.mean(x * x, axis=-1, keepdims=True) + NORM_EPS)) * g


def _head_norm_kernel(x_ref, g_ref, *rest, rope, heads):
    o_ref = rest[-1]
    for h in range(heads):
        cols = slice(h * HEAD_DIM, (h + 1) * HEAD_DIM)
        y = _rms_head(x_ref[:, cols], g_ref[:, cols])
        if rope:
            cos_ref, sin_ref = rest[:2]
            y = y * cos_ref[...] + _swap_rope_halves(y) * sin_ref[...]
        o_ref[:, cols] = y.astype(o_ref.dtype)


def _head_norm(src, first_head, gains, rope_tables=None):
    t = src.shape[0]
    n_heads = gains.shape[1] // HEAD_DIM
    heads = HEAD_NORM_GROUP if first_head % HEAD_NORM_GROUP == 0 and n_heads % HEAD_NORM_GROUP == 0 else 1
    width = heads * HEAD_DIM
    seq = rope_tables[0].shape[0] if rope_tables is not None else t
    tm = _tile(seq, 1024, 16)
    per_seq = seq // tm
    in_specs = [pl.BlockSpec((tm, width), lambda i, h: (i, first_head // heads + h)),
                pl.BlockSpec((1, width), lambda i, h: (0, h))]
    args = [src, gains]
    if rope_tables is not None:
        in_specs += [pl.BlockSpec((tm, HEAD_DIM), lambda i, h: (i % per_seq, 0))] * 2
        args += list(rope_tables)
    blocks = 2 * _nbytes((tm, width), BF16) + 2 * _nbytes((tm, HEAD_DIM), F32)
    return pl.pallas_call(
        functools.partial(_head_norm_kernel, rope=rope_tables is not None, heads=heads),
        grid=(t // tm, n_heads // heads),
        in_specs=in_specs,
        out_specs=pl.BlockSpec((tm, width), lambda i, h: (i, h)),
        out_shape=jax.ShapeDtypeStruct((t, n_heads * HEAD_DIM), BF16),
        compiler_params=_params(("parallel", "parallel"), blocks, 6 * heads * _nbytes((tm, HEAD_DIM), F32)),
        name="head_norm_rope" if rope_tables is not None else "head_norm",
    )(*args)


def _rope_tables(seq):
    half = HEAD_DIM // 2
    t = jnp.arange(seq)
    rows = (t // GRID_W).astype(F32)
    cols = (t % GRID_W).astype(F32)
    inv = 1.0 / (ROPE_THETA ** (jnp.arange(0, half, 2, dtype=F32) / half))
    ang = jnp.concatenate([rows[:, None] * inv, rows[:, None] * inv, cols[:, None] * inv, cols[:, None] * inv], -1)
    sign = jnp.where(jnp.arange(HEAD_DIM) % half < half // 2, -1.0, 1.0).astype(F32)
    return jnp.cos(ang), jnp.sin(ang) * sign


def _nt_dot(a, b):
    return lax.dot_general(a, b, (((1,), (1,)), ((), ())), preferred_element_type=F32)


def _sink_column(sink_ref, first, group, rows_per_head):
    head = lax.broadcasted_iota(jnp.int32, (group * rows_per_head, 1), 0) // rows_per_head
    col = jnp.zeros((group * rows_per_head, 1), F32)
    for g in range(group):
        col = jnp.where(head == g, sink_ref[first + g], col)
    return col


def _swa_kernel(sink_ref, q_ref, k_ref, v_ref, kc_ref, vc_ref, o_ref, *, tq, seq, group):
    kv_head, q_tile = pl.program_id(1), pl.program_id(2)
    kc, vc = kc_ref[0], vc_ref[0]
    sink = _sink_column(sink_ref, kv_head * group, group, BLOCK)
    q_off = lax.broadcasted_iota(jnp.int32, (group * BLOCK, 1), 0) % BLOCK
    k_off = lax.broadcasted_iota(jnp.int32, (1, 3 * BLOCK), 1)

    def scores(j):
        base = q_tile * tq + j * BLOCK
        start = pl.multiple_of(jnp.clip(base - BLOCK, 0, seq - 3 * BLOCK), BLOCK)
        q_rows = pl.ds(pl.multiple_of(j * BLOCK, BLOCK), BLOCK)
        q = jnp.concatenate([q_ref[0, q_rows, g * HEAD_DIM:(g + 1) * HEAD_DIM] for g in range(group)], axis=0)
        s_w = _nt_dot(q, k_ref[0, pl.ds(start, 3 * BLOCK), :])
        return base, start, q_rows, s_w, _nt_dot(q, kc)

    def attend(base, start, q_rows, s_w, s_c):
        s_w = jnp.where(jnp.abs((start + k_off) - (base + q_off)) <= WINDOW, s_w, NEG_INF)
        m = jnp.maximum(jnp.maximum(jnp.max(s_w, -1, keepdims=True), jnp.max(s_c, -1, keepdims=True)), sink)
        p_w = jnp.exp(s_w - m)
        p_c = jnp.exp(s_c - m)
        den = jnp.sum(p_w, -1, keepdims=True) + jnp.sum(p_c, -1, keepdims=True) + jnp.exp(sink - m)
        o = (jnp.dot(p_w.astype(BF16), v_ref[0, pl.ds(start, 3 * BLOCK), :], preferred_element_type=F32)
             + jnp.dot(p_c.astype(BF16), vc, preferred_element_type=F32)) / den
        for g in range(group):
            o_ref[0, q_rows, g * HEAD_DIM:(g + 1) * HEAD_DIM] = o[g * BLOCK:(g + 1) * BLOCK].astype(o_ref.dtype)

    def block_group(i, carry):
        staged = [scores(i * SWA_BLOCK_UNROLL + u) for u in range(SWA_BLOCK_UNROLL)]
        for block_state in staged:
            attend(*block_state)
        return carry

    assert (tq // BLOCK) % SWA_BLOCK_UNROLL == 0
    lax.fori_loop(0, tq // BLOCK // SWA_BLOCK_UNROLL, block_group, 0)


def _swa(qk, proj, kc, projc, sink, *, n_heads, v_col, batch, seq):
    group = n_heads // SWA_KV_HEADS
    lc = kc.shape[0] // batch
    gw = group * HEAD_DIM
    tq = _tile(seq, 1024, BLOCK)
    qk3 = qk.reshape(batch, seq, qk.shape[1])
    proj3 = proj.reshape(batch, seq, proj.shape[1])
    kc3 = kc.reshape(batch, lc, kc.shape[1])
    projc3 = projc.reshape(batch, lc, projc.shape[1])
    blocks = (2 * _nbytes((tq, gw), BF16) + 2 * _nbytes((seq, HEAD_DIM), BF16) + 2 * _nbytes((lc, HEAD_DIM), BF16))
    temps = 6 * _nbytes((group * BLOCK, 3 * BLOCK + lc), F32)
    out = pl.pallas_call(
        functools.partial(_swa_kernel, tq=tq, seq=seq, group=group),
        grid=(batch, SWA_KV_HEADS, seq // tq),
        in_specs=[pl.BlockSpec(memory_space=pltpu.SMEM),
                  pl.BlockSpec((1, tq, gw), lambda b, h, i: (b, i, h)),
                  pl.BlockSpec((1, seq, HEAD_DIM), lambda b, h, i: (b, 0, n_heads + h)),
                  pl.BlockSpec((1, seq, HEAD_DIM), lambda b, h, i: (b, 0, v_col + h)),
                  pl.BlockSpec((1, lc, HEAD_DIM), lambda b, h, i: (b, 0, h)),
                  pl.BlockSpec((1, lc, HEAD_DIM), lambda b, h, i: (b, 0, v_col + h))],
        out_specs=pl.BlockSpec((1, tq, gw), lambda b, h, i: (b, i, h)),
        out_shape=jax.ShapeDtypeStruct((batch, seq, n_heads * HEAD_DIM), BF16),
        compiler_params=_params(("parallel", "parallel", "parallel"), blocks, temps),
        name="window_attention",
    )(sink, qk3, qk3, proj3, kc3, projc3)
    return out.reshape(batch * seq, n_heads * HEAD_DIM)


def _ctx_attn_kernel(*refs, group, lc, use_sink):
    if use_sink:
        sink_ref, q_ref, k_ref, v_ref, o_ref = refs
    else:
        q_ref, k_ref, v_ref, o_ref = refs
    q = jnp.concatenate([q_ref[0, :, g * HEAD_DIM:(g + 1) * HEAD_DIM] for g in range(group)], axis=0)
    s = _nt_dot(q, k_ref[0])
    m = jnp.max(s, -1, keepdims=True)
    if use_sink:
        sink = _sink_column(sink_ref, pl.program_id(1) * group, group, lc)
        m = jnp.maximum(m, sink)
    p = jnp.exp(s - m)
    den = jnp.sum(p, -1, keepdims=True)
    if use_sink:
        den = den + jnp.exp(sink - m)
    o = jnp.dot(p.astype(BF16), v_ref[0], preferred_element_type=F32) / den
    for g in range(group):
        o_ref[0, :, g * HEAD_DIM:(g + 1) * HEAD_DIM] = o[g * lc:(g + 1) * lc].astype(o_ref.dtype)


def _ctx_attn(qc, kc, projc, sink, *, n_heads, n_kv, v_col, batch):
    group = n_heads // n_kv
    lc = qc.shape[0] // batch
    gw = group * HEAD_DIM
    qc3 = qc.reshape(batch, lc, qc.shape[1])
    kc3 = kc.reshape(batch, lc, kc.shape[1])
    projc3 = projc.reshape(batch, lc, projc.shape[1])
    use_sink = sink is not None
    in_specs = [pl.BlockSpec((1, lc, gw), lambda b, h: (b, 0, h)),
                pl.BlockSpec((1, lc, HEAD_DIM), lambda b, h: (b, 0, h)),
                pl.BlockSpec((1, lc, HEAD_DIM), lambda b, h: (b, 0, v_col + h))]
    args = [qc3, kc3, projc3]
    if use_sink:
        in_specs = [pl.BlockSpec(memory_space=pltpu.SMEM)] + in_specs
        args = [sink] + args
    blocks = 2 * _nbytes((lc, gw), BF16) + 2 * _nbytes((lc, HEAD_DIM), BF16)
    out = pl.pallas_call(
        functools.partial(_ctx_attn_kernel, group=group, lc=lc, use_sink=use_sink),
        grid=(batch, n_kv),
        in_specs=in_specs,
        out_specs=pl.BlockSpec((1, lc, gw), lambda b, h: (b, 0, h)),
        out_shape=jax.ShapeDtypeStruct((batch, lc, n_heads * HEAD_DIM), BF16),
        compiler_params=_params(("parallel", "parallel"), blocks, 6 * _nbytes((group * lc, lc), F32)),
        name="context_attention",
    )(*args)
    return out.reshape(batch * lc, n_heads * HEAD_DIM)


def _nat_bias_kernel(rpb_ref, o_ref):
    h = pl.program_id(0)
    n_dc = 2 * NA_KW - 1
    w = lax.broadcasted_iota(jnp.int32, (GRID_W, 2 * GRID_W), 0)
    lane = lax.broadcasted_iota(jnp.int32, (GRID_W, 2 * GRID_W), 1)
    second = lane >= GRID_W
    kc = lane % GRID_W
    c0 = jnp.clip(w - NA_KW // 2, 0, GRID_W - NA_KW)
    dc = jnp.where((kc >= c0) & (kc < c0 + NA_KW), kc - w + (NA_KW - 1), -1)

    def pair(dr, carry):
        base = (h * (2 * NA_KH - 1) + dr) * n_dc
        tile = jnp.full((GRID_W, 2 * GRID_W), NEG_INF, F32)
        for j in range(n_dc):
            tile = jnp.where(dc == j, jnp.where(second, rpb_ref[base + n_dc + j], rpb_ref[base + j]), tile)
        o_ref[0, dr] = tile
        return carry

    lax.fori_loop(0, 2 * NA_KH - 2, pair, 0)


def _nat_bias(rpb):
    n_heads = rpb.shape[0]
    assert rpb.shape[1:] == (2 * NA_KH - 1, 2 * NA_KW - 1)
    shape = (n_heads, 2 * NA_KH - 2, GRID_W, 2 * GRID_W)
    return pl.pallas_call(
        _nat_bias_kernel,
        grid=(n_heads,),
        in_specs=[pl.BlockSpec(memory_space=pltpu.SMEM)],
        out_specs=pl.BlockSpec((1,) + shape[1:], lambda h: (h, 0, 0, 0)),
        out_shape=jax.ShapeDtypeStruct(shape, F32),
        compiler_params=_params(("parallel",), _nbytes(shape[1:], F32)),
        name="nat_bias_tiles",
    )(rpb.reshape(-1))


def _nat_kernel(q_ref, k_ref, v_ref, kc_ref, vc_ref, gq_ref, gk_ref, bias_ref, o_ref, kn_ref, *, rows, norm_rows,
                heads, group_rows):
    n_pairs = NA_KH // 2
    gq = gq_ref[...] * ATTN_SCALE
    gk = gk_ref[...]
    n_ctx = kc_ref.shape[1] // HEAD_DIM

    for head in range(heads):
        cols = slice(head * HEAD_DIM, (head + 1) * HEAD_DIM)

        def norm_keys(i, carry, cols=cols):
            chunk = pl.ds(pl.multiple_of(i * norm_rows, norm_rows), norm_rows)
            kn_ref[chunk, :] = _rms_head(k_ref[0, chunk, cols], gk).astype(BF16)
            return carry

        lax.fori_loop(0, rows * GRID_W // norm_rows, norm_keys, 0, unroll=2)
        kc = _rms_head(kc_ref[0, :, cols], gk).astype(BF16)
        vc = vc_ref[0, :, cols]

        def window(r):
            r0 = jnp.clip(r - NA_KH // 2, 0, rows - NA_KH)
            q_rows = pl.ds(pl.multiple_of(r * GRID_W, GRID_W), GRID_W)
            k_rows = pl.ds(pl.multiple_of(r0 * GRID_W, GRID_W), NA_KH * GRID_W)
            return q_rows, k_rows, r0 - r + (NA_KH - 1)

        def logits(dr0, s_nb, s_c, head=head):
            s = [s_nb[:, 2 * GRID_W * j:2 * GRID_W * (j + 1)] + bias_ref[head, dr0 + 2 * j] for j in range(n_pairs)]
            s += [s_c[:, HEAD_DIM * j:HEAD_DIM * (j + 1)] for j in range(n_ctx)]
            return s, jnp.max(functools.reduce(jnp.maximum, s), -1, keepdims=True)

        def row_group(i, carry, cols=cols, kc=kc, vc=vc, window=window, logits=logits):
            wins = [window(i * group_rows + u) for u in range(group_rows)]
            q = [_rms_head(q_ref[0, q_rows, cols], gq).astype(BF16) for q_rows, _, _ in wins]
            s_c = _nt_dot(jnp.concatenate(q, axis=0), kc)
            s_nb = [_nt_dot(qu, kn_ref[k_rows, :]) for qu, (_, k_rows, _) in zip(q, wins)]
            staged = [logits(dr0, s_nb[u], s_c[u * GRID_W:(u + 1) * GRID_W]) for u, (_, _, dr0) in enumerate(wins)]
            p = [[jnp.exp(sj - m) for sj in s] for s, m in staged]
            den = [jnp.sum(functools.reduce(jnp.add, pu), -1, keepdims=True) for pu in p]
            o_nb = [jnp.dot(jnp.concatenate(pu[:n_pairs], axis=-1).astype(BF16), v_ref[0, k_rows, cols],
                            preferred_element_type=F32) for pu, (_, k_rows, _) in zip(p, wins)]
            p_c = [jnp.concatenate(pu[n_pairs:], axis=-1).astype(BF16) for pu in p]
            o_c = jnp.dot(jnp.concatenate(p_c, axis=0), vc, preferred_element_type=F32)
            for u, (q_rows, _, _) in enumerate(wins):
                o = (o_nb[u] + o_c[u * GRID_W:(u + 1) * GRID_W]) / den[u]
                o_ref[0, q_rows, cols] = o.astype(o_ref.dtype)
            return carry

        lax.fori_loop(0, rows // group_rows, row_group, 0)


def _nat(proj, projc, gq, gk, bias, *, n_heads, batch, seq):
    rows = seq // GRID_W
    assert seq % GRID_W == 0 and rows >= NA_KH
    group_rows = math.gcd(rows, NAT_ROW_UNROLL)
    lc = projc.shape[0] // batch
    assert lc % HEAD_DIM == 0
    heads = NAT_HEADS_PER_STEP if n_heads % NAT_HEADS_PER_STEP == 0 else 1
    width = heads * HEAD_DIM
    norm_rows = _tile(seq, 512, 16)
    proj3 = proj.reshape(batch, seq, proj.shape[1])
    projc3 = projc.reshape(batch, lc, projc.shape[1])

    def head_block(n, first_head):
        return pl.BlockSpec((1, n, width), lambda b, h: (b, 0, first_head // heads + h))

    gain_block = pl.BlockSpec((1, HEAD_DIM), lambda b, h: (0, 0))
    bias_block = (heads,) + bias.shape[1:]
    blocks = 4 * _nbytes((seq, width), BF16) + 2 * _nbytes((lc, width), BF16) + _nbytes(bias_block, F32)
    temps = (_nbytes((seq, HEAD_DIM), BF16)
             + 4 * group_rows * _nbytes((GRID_W, NA_KH * GRID_W + lc), F32) + 4 * _nbytes((norm_rows, HEAD_DIM), F32))
    out = pl.pallas_call(
        functools.partial(_nat_kernel, rows=rows, norm_rows=norm_rows, heads=heads, group_rows=group_rows),
        grid=(batch, n_heads // heads),
        in_specs=[head_block(seq, 0), head_block(seq, n_heads), head_block(seq, 2 * n_heads),
                  head_block(lc, n_heads), head_block(lc, 2 * n_heads),
                  gain_block, gain_block,
                  pl.BlockSpec(bias_block, lambda b, h: (h, 0, 0, 0))],
        out_specs=head_block(seq, 0),
        out_shape=jax.ShapeDtypeStruct((batch, seq, n_heads * HEAD_DIM), BF16),
        scratch_shapes=[pltpu.VMEM((seq, HEAD_DIM), BF16)],
        compiler_params=_params(("parallel", "parallel"), blocks, temps),
        name="neighbourhood_attention",
    )(proj3, proj3, proj3, projc3, projc3, gq.reshape(1, HEAD_DIM), gk.reshape(1, HEAD_DIM), bias)
    return out.reshape(batch * seq, n_heads * HEAD_DIM)


def _gains(*groups):
    return jnp.concatenate([jnp.tile(g.astype(F32), n) for g, n in groups]).reshape(1, -1)


def kernel(x, c, ctx, c_ctx, w_ada, b_ada, norm_g, ffn_w_gate, ffn_w_up, ffn_w_down, mix_ab_w_in, mix_ab_w_out,
           gm_vnorm_g, gm_w_s, gm_b_s, swa_qnorm_g, swa_knorm_g, swa_sink, nat_w_qkv, nat_w_out, nat_qnorm_g,
           nat_knorm_g, nat_rpb):
    batch, seq, d = x.shape
    lc = ctx.shape[1]
    depth = w_ada.shape[0]
    ctx_row = batch
    assert batch + 1 <= MOD_ROWS

    cond = jnp.zeros((MOD_ROWS, d), F32).at[:batch].set(c).at[ctx_row].set(c_ctx)
    mod = _ada(cond, w_ada, b_ada).reshape(depth, MOD_ROWS, N_MOD, 1, d)

    lat_row = _ModRows(0, seq)
    ctx_row_of = _ModRows(ctx_row, batch * lc)

    def mod_vec(layer, sub, which):
        return mod[layer, :, 3 * sub + which]

    def ffn_in(xs, row, layer, sub):
        return _norm_mod(xs, norm_g[layer, sub], mod_vec(layer, sub, 0), mod_vec(layer, sub, 1), row)

    def ffn_out(a, wd, xs, row, layer, sub):
        return _residual_matmul([a], wd, xs, mod_vec(layer, sub, 2), row, HALF_STEP)

    def ffn_ctx(cs, layer, sub, wg, wu, wd):
        a, _ = _gate_up(ffn_in(cs, ctx_row_of, layer, sub), wg, wu)
        return ffn_out(a, wd, cs, ctx_row_of, layer, sub)

    def ffn_f32(layer, idx):
        return (_Weight(w, (layer, idx)) for w in (ffn_w_gate, ffn_w_up, ffn_w_down))

    xs = x.reshape(batch * seq, d)
    cs = ctx.reshape(batch * lc, d)
    rope = _rope_tables(seq)

    wg0_f32, wu0_f32, _ = ffn_f32(0, 0)
    wg, wu = (_Weight(w.array[w.lead].astype(BF16)) for w in (wg0_f32, wu0_f32))

    for layer in range(depth):
        last = layer == depth - 1
        j = layer // 2
        even = layer % 2 == 0
        mix_in_f32 = _Weight(mix_ab_w_in if even else nat_w_qkv, (j,))
        mix_out_f32 = _Weight(mix_ab_w_out if even else nat_w_out, (j,))
        _, _, wd_f32 = ffn_f32(layer, 0)
        wg1_f32, wu1_f32, wd1_f32 = ffn_f32(layer, 1)
        a, (wd, mix_in, mix_out, wg1, wu1) = _gate_up(ffn_in(xs, lat_row, layer, 0), wg, wu,
                                                      [wd_f32, mix_in_f32, mix_out_f32, wg1_f32, wu1_f32])
        xs = ffn_out(a, wd, xs, lat_row, layer, 0)
        cs = ffn_ctx(cs, layer, 0, wg, wu, wd)
        wg, wu = wg1, wu1
        h = _norm_mod(xs, norm_g[layer, 1], mod_vec(layer, 1, 0), mod_vec(layer, 1, 1), lat_row)
        hc = _norm_mod(cs, norm_g[layer, 1], mod_vec(layer, 1, 0), mod_vec(layer, 1, 1), ctx_row_of)
        gate = mod_vec(layer, 1, 2)
        proj = _matmul(h, mix_in)
        projc = _matmul(hc, mix_in)
        if even:
            n_groups = gm_w_s.shape[1]
            n_heads = swa_sink.shape[1]
            assert n_groups == n_heads
            q_col = 2 * n_groups
            v_col = q_col + n_heads + SWA_KV_HEADS
            qk_gain = _gains((swa_qnorm_g[j] * ATTN_SCALE, n_heads), (swa_knorm_g[j], SWA_KV_HEADS))
            qk = _head_norm(proj, q_col, qk_gain, rope)
            qkc = _head_norm(projc, q_col, qk_gain)
            kc = qkc[:, n_heads * HEAD_DIM:]
            y_g = _gmlp(proj, gm_w_s[j], gm_b_s[j], gm_vnorm_g[j])
            y_a = _swa(qk, proj, kc, projc, swa_sink[j], n_heads=n_heads, v_col=v_col, batch=batch, seq=seq)
            xs = _residual_matmul([y_g, y_a], mix_out, xs, gate, lat_row, 1.0)
            if not last:
                yc_g = _gmlp(projc, gm_w_s[j], gm_b_s[j], gm_vnorm_g[j])
                yc_a = _ctx_attn(qkc, kc, projc, swa_sink[j], n_heads=n_heads, n_kv=SWA_KV_HEADS, v_col=v_col,
                                 batch=batch)
                cs = _residual_matmul([yc_g, yc_a], mix_out, cs, gate, ctx_row_of, 1.0)
        else:
            n_heads = nat_rpb.shape[1]
            y = _nat(proj, projc, nat_qnorm_g[j], nat_knorm_g[j], _nat_bias(nat_rpb[j]), n_heads=n_heads,
                     batch=batch, seq=seq)
            xs = _residual_matmul([y], mix_out, xs, gate, lat_row, 1.0)
            if not last:
                qk_gain = _gains((nat_qnorm_g[j] * ATTN_SCALE, n_heads), (nat_knorm_g[j], n_heads))
                qkc = _head_norm(projc, 0, qk_gain)
                yc = _ctx_attn(qkc, qkc[:, n_heads * HEAD_DIM:], projc, None, n_heads=n_heads, n_kv=n_heads,
                               v_col=2 * n_heads, batch=batch)
                cs = _residual_matmul([yc], mix_out, cs, gate, ctx_row_of, 1.0)
        next_f32 = [] if last else list(ffn_f32(layer + 1, 0))[:2]
        a, (wd1, *next_w) = _gate_up(ffn_in(xs, lat_row, layer, 2), wg, wu, [wd1_f32] + next_f32)
        xs = ffn_out(a, wd1, xs, lat_row, layer, 2)
        if not last:
            cs = ffn_ctx(cs, layer, 2, wg, wu, wd1)
            wg, wu = next_w
    return xs.reshape(batch, seq, d)
```
